```python
import math
import jax, jax.numpy as jnp
from jax import lax
import numpy as np

D_MODEL = 2048
BATCH = 4
SEQ = 4096
DEPTH = 1

HEAD_DIM = 128
N_ATTN_HEADS = 8
ATTN_WIDTH = N_ATTN_HEADS * HEAD_DIM
MOBA_BLOCK = 256
MOBA_TOPK = 3
Q_CHUNK = 32
ROPE_THETA = 10000.0
SSM_WIDTH = D_MODEL // 2
SSM_GROUP = 16
SSM_GROUPS = SSM_WIDTH // SSM_GROUP
SSM_STATE = 64
FFN_HIDDEN = 5632
CONV_WIDTH = 3
N_BRANCHES = 2
N_MOD = 6
IN_WIDTH = 3 * ATTN_WIDTH + SSM_WIDTH + N_BRANCHES * D_MODEL
EPS = 1e-6
NEG_INF = -1e30

kernel_name = 'moba_s5_gated_hybrid_block'


def rms_norm(x, g):
    xf = x.astype(jnp.float32)
    y = xf * lax.rsqrt(jnp.mean(xf * xf, axis=-1, keepdims=True) + EPS)
    return y.astype(x.dtype) * g


def rotary(x, positions):
    half = HEAD_DIM // 2
    inv_freq = ROPE_THETA ** (-jnp.arange(half, dtype=jnp.float32) / half)
    ang = positions.astype(jnp.float32)[..., None] * inv_freq
    cos = jnp.cos(ang)[:, :, None, :].astype(x.dtype)
    sin = jnp.sin(ang)[:, :, None, :].astype(x.dtype)
    x1, x2 = x[..., :half], x[..., half:]
    return jnp.concatenate([x1 * cos - x2 * sin, x1 * sin + x2 * cos], axis=-1)


def moba_attention(q, k, v):
    b, h, s, dh = q.shape
    n_blocks = -(-s // MOBA_BLOCK)
    s_pad = n_blocks * MOBA_BLOCK
    topk = min(MOBA_TOPK, n_blocks)
    pad = ((0, 0), (0, 0), (0, s_pad - s), (0, 0))
    k_p = jnp.pad(k, pad)
    v_p = jnp.pad(v, pad)
    k_blocks = k_p.reshape(b, h, n_blocks, MOBA_BLOCK, dh)
    v_blocks = v_p.reshape(b, h, n_blocks, MOBA_BLOCK, dh)
    k_mean = jnp.mean(k_blocks.astype(jnp.float32), axis=3).astype(k.dtype)
    scale = HEAD_DIM ** -0.5
    gather = jax.vmap(jax.vmap(lambda blocks, idx: blocks[idx]))

    def chunk(ci):
        start = ci * Q_CHUNK
        q_c = lax.dynamic_slice_in_dim(q, start, Q_CHUNK, axis=2)
        t = start + jnp.arange(Q_CHUNK)
        cur = start // MOBA_BLOCK
        blk_scores = jnp.einsum('bhqd,bhnd->bhqn', q_c, k_mean).astype(jnp.float32)
        past = jnp.arange(n_blocks) < cur
        blk_scores = jnp.where(past, blk_scores, NEG_INF)
        _, idx = lax.top_k(blk_scores, topk)
        slot_ok = jnp.arange(topk) < cur
        k_sel = gather(k_blocks, idx)
        v_sel = gather(v_blocks, idx)
        s_sel = jnp.einsum('bhqd,bhqnjd->bhqnj', q_c, k_sel).astype(jnp.float32) * scale
        s_sel = jnp.where(slot_ok[:, None], s_sel, NEG_INF).reshape(b, h, Q_CHUNK, topk * MOBA_BLOCK)
        k_own = lax.dynamic_slice_in_dim(k_p, cur * MOBA_BLOCK, MOBA_BLOCK, axis=2)
        v_own = lax.dynamic_slice_in_dim(v_p, cur * MOBA_BLOCK, MOBA_BLOCK, axis=2)
        s_own = jnp.einsum('bhqd,bhjd->bhqj', q_c, k_own).astype(jnp.float32) * scale
        key_pos = cur * MOBA_BLOCK + jnp.arange(MOBA_BLOCK)
        s_own = jnp.where(key_pos[None, :] <= t[:, None], s_own, NEG_INF)
        p = jax.nn.softmax(jnp.concatenate([s_sel, s_own], axis=-1), axis=-1).astype(v.dtype)
        p_sel = p[..., :topk * MOBA_BLOCK].reshape(b, h, Q_CHUNK, topk, MOBA_BLOCK)
        p_own = p[..., topk * MOBA_BLOCK:]
        return (jnp.einsum('bhqnj,bhqnjd->bhqd', p_sel, v_sel)
                + jnp.einsum('bhqj,bhjd->bhqd', p_own, v_own))

    out = lax.map(chunk, jnp.arange(s // Q_CHUNK))
    return out.transpose(1, 2, 0, 3, 4).reshape(b, h, s, dh)


def s5_branch(u, lam_re, lam_im, log_dt, b_re, b_im, c_re, c_im, d_skip, w_glu, b_glu):
    bsz, s, _ = u.shape
    f32 = jnp.float32
    ug = u.reshape(bsz, s, SSM_GROUPS, SSM_GROUP).astype(f32)
    dt = jnp.exp(log_dt.astype(f32))[:, None]
    lr = lam_re.astype(f32)
    li = lam_im.astype(f32)
    mag = jnp.exp(lr * dt)
    ab_re = mag * jnp.cos(li * dt)
    ab_im = mag * jnp.sin(li * dt)
    den = lr * lr + li * li
    nr = ab_re - 1.0
    ni = ab_im
    coef_re = (nr * lr + ni * li) / den
    coef_im = (ni * lr - nr * li) / den
    bu_re = jnp.einsum('bsgh,gph->bsgp', ug, b_re.astype(f32))
    bu_im = jnp.einsum('bsgh,gph->bsgp', ug, b_im.astype(f32))
    x_re0 = coef_re * bu_re - coef_im * bu_im
    x_im0 = coef_re * bu_im + coef_im * bu_re
    a_re = jnp.broadcast_to(ab_re, x_re0.shape)
    a_im = jnp.broadcast_to(ab_im, x_re0.shape)

    def combine(e1, e2):
        a1r, a1i, b1r, b1i = e1
        a2r, a2i, b2r, b2i = e2
        return (a2r * a1r - a2i * a1i, a2r * a1i + a2i * a1r,
                a2r * b1r - a2i * b1i + b2r, a2r * b1i + a2i * b1r + b2i)

    _, _, xr, xi = lax.associative_scan(combine, (a_re, a_im, x_re0, x_im0), axis=1)
    y = (jnp.einsum('bsgp,ghp->bsgh', xr, c_re.astype(f32))
         - jnp.einsum('bsgp,ghp->bsgh', xi, c_im.astype(f32))
         + d_skip.astype(f32).reshape(SSM_GROUPS, SSM_GROUP) * ug)
    y = jax.nn.gelu(y.reshape(bsz, s, SSM_WIDTH)).astype(u.dtype)
    return y * jax.nn.sigmoid(y @ w_glu + b_glu)


def causal_dwconv(u, w, bias):
    ch = u.shape[-1]
    y = lax.conv_general_dilated(u, w[:, None, :].astype(u.dtype), window_strides=(1,),
                                 padding=[(CONV_WIDTH - 1, 0)],
                                 dimension_numbers=('NWC', 'WIO', 'NWC'),
                                 feature_group_count=ch)
    return y + bias


def setup_inputs(seed: int = 0) -> dict:
    key = jax.random.key(seed)
    ks = jax.random.split(key, 32)
    nrm = jax.random.normal
    L = DEPTH
    x = nrm(ks[0], (BATCH, SEQ, D_MODEL), jnp.float32)
    c = nrm(ks[1], (BATCH, D_MODEL), jnp.float32)
    offsets = jax.random.randint(ks[2], (BATCH, 1), 0, 1024, dtype=jnp.int32)
    positions = offsets + jnp.arange(SEQ, dtype=jnp.int32)[None, :]
    w_mod = nrm(ks[3], (L, D_MODEL, N_MOD * D_MODEL), jnp.float32) * (0.5 * D_MODEL ** -0.5)
    b_mod = nrm(ks[4], (L, N_MOD * D_MODEL), jnp.float32) * 0.01
    norm1_g = 1.0 + 0.02 * nrm(ks[5], (L, D_MODEL), jnp.float32)
    w_in = nrm(ks[6], (L, D_MODEL, IN_WIDTH), jnp.float32) * D_MODEL ** -0.5
    q_norm_g = 1.0 + 0.02 * nrm(ks[7], (L, HEAD_DIM), jnp.float32)
    k_norm_g = 1.0 + 0.02 * nrm(ks[8], (L, HEAD_DIM), jnp.float32)
    n_idx = jnp.arange(SSM_STATE, dtype=jnp.float32)
    ssm_lambda_re = -0.5 + 0.01 * nrm(ks[9], (L, SSM_GROUPS, SSM_STATE), jnp.float32)
    ssm_lambda_im = math.pi * n_idx + 0.01 * nrm(ks[10], (L, SSM_GROUPS, SSM_STATE), jnp.float32)
    ssm_log_dt = jax.random.uniform(ks[11], (L, SSM_GROUPS), jnp.float32, math.log(1e-3), math.log(1e-1))
    b_scale = (2.0 * SSM_GROUP) ** -0.5
    ssm_b_re = nrm(ks[12], (L, SSM_GROUPS, SSM_STATE, SSM_GROUP), jnp.float32) * b_scale
    ssm_b_im = nrm(ks[13], (L, SSM_GROUPS, SSM_STATE, SSM_GROUP), jnp.float32) * b_scale
    c_scale = (2.0 * SSM_STATE) ** -0.5
    ssm_c_re = nrm(ks[14], (L, SSM_GROUPS, SSM_GROUP, SSM_STATE), jnp.float32) * c_scale
    ssm_c_im = nrm(ks[15], (L, SSM_GROUPS, SSM_GROUP, SSM_STATE), jnp.float32) * c_scale
    ssm_d = nrm(ks[16], (L, SSM_WIDTH), jnp.float32)
    w_glu = nrm(ks[17], (L, SSM_WIDTH, SSM_WIDTH), jnp.float32) * SSM_WIDTH ** -0.5
    b_glu = nrm(ks[18], (L, SSM_WIDTH), jnp.float32) * 0.01
    w_attn_br = nrm(ks[19], (L, ATTN_WIDTH, D_MODEL), jnp.float32) * ATTN_WIDTH ** -0.5
    w_ssm_br = nrm(ks[20], (L, SSM_WIDTH, D_MODEL), jnp.float32) * SSM_WIDTH ** -0.5
    w_out = nrm(ks[21], (L, D_MODEL, D_MODEL), jnp.float32) * D_MODEL ** -0.5
    norm2_g = 1.0 + 0.02 * nrm(ks[22], (L, D_MODEL), jnp.float32)
    w_up = nrm(ks[23], (L, D_MODEL, 2 * FFN_HIDDEN), jnp.float32) * D_MODEL ** -0.5
    conv_w = nrm(ks[24], (L, CONV_WIDTH, 2 * FFN_HIDDEN), jnp.float32) * CONV_WIDTH ** -0.5
    conv_b = nrm(ks[25], (L, 2 * FFN_HIDDEN), jnp.float32) * 0.01
    w_down = nrm(ks[26], (L, FFN_HIDDEN, D_MODEL), jnp.float32) * FFN_HIDDEN ** -0.5
    return {'x': x, 'c': c, 'positions': positions, 'w_mod': w_mod, 'b_mod': b_mod,
            'norm1_g': norm1_g, 'w_in': w_in, 'q_norm_g': q_norm_g, 'k_norm_g': k_norm_g,
            'ssm_lambda_re': ssm_lambda_re, 'ssm_lambda_im': ssm_lambda_im, 'ssm_log_dt': ssm_log_dt,
            'ssm_b_re': ssm_b_re, 'ssm_b_im': ssm_b_im, 'ssm_c_re': ssm_c_re, 'ssm_c_im': ssm_c_im,
            'ssm_d': ssm_d, 'w_glu': w_glu, 'b_glu': b_glu, 'w_attn_br': w_attn_br,
            'w_ssm_br': w_ssm_br, 'w_out': w_out, 'norm2_g': norm2_g, 'w_up': w_up,
            'conv_w': conv_w, 'conv_b': conv_b, 'w_down': w_down}


def reference(x, c, positions, w_mod, b_mod, norm1_g, w_in, q_norm_g, k_norm_g,
              ssm_lambda_re, ssm_lambda_im, ssm_log_dt, ssm_b_re, ssm_b_im, ssm_c_re, ssm_c_im,
              ssm_d, w_glu, b_glu, w_attn_br, w_ssm_br, w_out, norm2_g, w_up, conv_w, conv_b,
              w_down):
    bsz, s, _ = x.shape
    split_pts = [ATTN_WIDTH, 2 * ATTN_WIDTH, 3 * ATTN_WIDTH, 3 * ATTN_WIDTH + SSM_WIDTH]
    for l in range(DEPTH):
        mod = jax.nn.silu(c) @ w_mod[l] + b_mod[l]
        sh1, sc1, g1, sh2, sc2, g2 = jnp.split(mod, N_MOD, axis=-1)

        h = rms_norm(x, norm1_g[l]) * (1.0 + sc1[:, None]) + sh1[:, None]
        proj = h @ w_in[l]
        q, k, v, u, gates = jnp.split(proj, split_pts, axis=-1)
        q = rotary(rms_norm(q.reshape(bsz, s, N_ATTN_HEADS, HEAD_DIM), q_norm_g[l]), positions)
        k = rotary(rms_norm(k.reshape(bsz, s, N_ATTN_HEADS, HEAD_DIM), k_norm_g[l]), positions)
        v = v.reshape(bsz, s, N_ATTN_HEADS, HEAD_DIM)
        attn = moba_attention(q.transpose(0, 2, 1, 3), k.transpose(0, 2, 1, 3), v.transpose(0, 2, 1, 3))
        y_attn = attn.transpose(0, 2, 1, 3).reshape(bsz, s, ATTN_WIDTH)
        y_ssm = s5_branch(u, ssm_lambda_re[l], ssm_lambda_im[l], ssm_log_dt[l], ssm_b_re[l], ssm_b_im[l],
                          ssm_c_re[l], ssm_c_im[l], ssm_d[l], w_glu[l], b_glu[l])
        gate_a, gate_s = jnp.split(gates, N_BRANCHES, axis=-1)
        merged = (jax.nn.sigmoid(gate_a) * (y_attn @ w_attn_br[l])
                  + jax.nn.sigmoid(gate_s) * (y_ssm @ w_ssm_br[l]))
        x = x + g1[:, None] * (merged @ w_out[l])

        h2 = rms_norm(x, norm2_g[l]) * (1.0 + sc2[:, None]) + sh2[:, None]
        up = causal_dwconv(h2 @ w_up[l], conv_w[l], conv_b[l])
        val, gt = jnp.split(up, 2, axis=-1)
        x = x + g2[:, None] * ((jax.nn.silu(gt) * val) @ w_down[l])
    return x
```

```python
import functools
import math

import jax
import jax.numpy as jnp
from jax import lax
from jax.experimental import pallas as pl
from jax.experimental.pallas import tpu as pltpu

F32 = jnp.float32
BF16 = jnp.bfloat16
HIGHEST = lax.Precision.HIGHEST

D_MODEL = 2048
BATCH = 4
SEQ = 4096
TOKENS = BATCH * SEQ
HEAD_DIM = 128
N_HEADS = 8
ATTN_WIDTH = N_HEADS * HEAD_DIM
MOBA_BLOCK = 256
N_KV_BLOCKS = SEQ // MOBA_BLOCK
MOBA_TOPK = 3
ROPE_THETA = 10000.0
SSM_WIDTH = D_MODEL // 2
SSM_GROUP = 16
SSM_GROUPS = SSM_WIDTH // SSM_GROUP
SSM_STATE = 64
FFN_HIDDEN = 5632
N_MOD = 6
IN_WIDTH = 3 * ATTN_WIDTH + SSM_WIDTH + 2 * D_MODEL
EPS = 1e-6
NEG_INF = -1e30

LANES = 128
SUBLANES = 8
VMEM_LIMIT = 56 * 1024 * 1024

TM_PROJ = 512
TN_PROJ = 1024
TM_MERGE = 256
TM_FFN = 512
TF_FFN = 512
N_F_TILES = FFN_HIDDEN // TF_FFN
FFN_HALO = 16
S5_TILE_GROUPS = LANES // SSM_GROUP
S5_TILES = SSM_GROUPS // S5_TILE_GROUPS
S5_TILE_STATES = S5_TILE_GROUPS * SSM_STATE
S5_CHUNK = 16
S5_CHUNKS = SEQ // S5_CHUNK
S5_FLAT = S5_CHUNK * LANES

COL_K = ATTN_WIDTH // LANES
COL_V = 2 * ATTN_WIDTH // LANES
COL_U = 3 * ATTN_WIDTH // LANES


def _params(*sem):
    return pltpu.CompilerParams(dimension_semantics=sem, vmem_limit_bytes=VMEM_LIMIT)


def _mod_kernel(c_ref, w_ref, b_ref, o_ref):
    c = c_ref[...]
    sc = c * jax.nn.sigmoid(c)
    o_ref[...] = jnp.dot(sc, w_ref[...], precision=HIGHEST, preferred_element_type=F32) + b_ref[...]


def _mod(c_pad, w_mod, b_mod):
    tn = 1024
    n = N_MOD * D_MODEL
    return pl.pallas_call(
        _mod_kernel,
        grid=(n // tn,),
        in_specs=[pl.BlockSpec((SUBLANES, D_MODEL), lambda j: (0, 0)),
                  pl.BlockSpec((D_MODEL, tn), lambda j: (0, j)),
                  pl.BlockSpec((1, tn), lambda j: (0, j))],
        out_specs=pl.BlockSpec((SUBLANES, tn), lambda j: (0, j)),
        out_shape=jax.ShapeDtypeStruct((SUBLANES, n), F32),
        compiler_params=_params("parallel"),
        name="mod",
    )(c_pad, w_mod, b_mod)


def _rope_kernel(pos_ref, freq_ref, cos_ref, sin_ref):
    ang = pos_ref[...].astype(F32) * freq_ref[...]
    lane = lax.broadcasted_iota(jnp.int32, ang.shape, 1)
    cos_ref[...] = jnp.cos(ang)
    sin_ref[...] = jnp.where(lane < HEAD_DIM // 2, -1.0, 1.0) * jnp.sin(ang)


def _rope_tables(pos_col, freq_row):
    tr = 2048
    return pl.pallas_call(
        _rope_kernel,
        grid=(TOKENS // tr,),
        in_specs=[pl.BlockSpec((tr, 1), lambda i: (i, 0)),
                  pl.BlockSpec((1, HEAD_DIM), lambda i: (0, 0))],
        out_specs=[pl.BlockSpec((tr, HEAD_DIM), lambda i: (i, 0)),
                   pl.BlockSpec((tr, HEAD_DIM), lambda i: (i, 0))],
        out_shape=[jax.ShapeDtypeStruct((TOKENS, HEAD_DIM), F32)] * 2,
        compiler_params=_params("parallel"),
        name="rope",
    )(pos_col, freq_row)


def _rms_modulate(x, g, scale, shift):
    ms = jnp.mean(x * x, axis=-1, keepdims=True)
    y = x * lax.rsqrt(ms + EPS) * g
    return y * (1.0 + scale) + shift


def _in_proj_kernel(x_ref, sc_ref, sh_ref, g_ref, w_ref, cos_ref, sin_ref, qg_ref, kg_ref,
                    o_ref, h_scr):
    j = pl.program_id(1)

    @pl.when(j == 0)
    def _():
        h_scr[...] = _rms_modulate(x_ref[...], g_ref[...], sc_ref[...], sh_ref[...]).astype(BF16)

    acc = jnp.dot(h_scr[...], w_ref[...], preferred_element_type=F32)

    @pl.when(j < 2)
    def _():
        g = jnp.where(j == 0, qg_ref[...], kg_ref[...])
        cos = cos_ref[...]
        sin = sin_ref[...]
        for hd in range(N_HEADS):
            blk = acc[:, hd * HEAD_DIM:(hd + 1) * HEAD_DIM]
            ms = jnp.mean(blk * blk, axis=-1, keepdims=True)
            y = blk * lax.rsqrt(ms + EPS) * g
            rot = y * cos + pltpu.roll(y, HEAD_DIM // 2, axis=1) * sin
            o_ref[:, hd * HEAD_DIM:(hd + 1) * HEAD_DIM] = rot.astype(BF16)

    @pl.when(j >= 2)
    def _():
        o_ref[...] = acc.astype(BF16)


def _in_proj(x2, mod3, norm1_g, w_in_bf, cos, sin, qg, kg):
    tiles_per_seq = SEQ // TM_PROJ
    return pl.pallas_call(
        _in_proj_kernel,
        grid=(TOKENS // TM_PROJ, IN_WIDTH // TN_PROJ),
        in_specs=[
            pl.BlockSpec((TM_PROJ, D_MODEL), lambda i, j: (i, 0)),
            pl.BlockSpec((None, 1, D_MODEL), lambda i, j: (i // tiles_per_seq * N_MOD + 1, 0, 0)),
            pl.BlockSpec((None, 1, D_MODEL), lambda i, j: (i // tiles_per_seq * N_MOD + 0, 0, 0)),
            pl.BlockSpec((1, D_MODEL), lambda i, j: (0, 0)),
            pl.BlockSpec((D_MODEL, TN_PROJ), lambda i, j: (0, j)),
            pl.BlockSpec((TM_PROJ, HEAD_DIM), lambda i, j: (i, 0)),
            pl.BlockSpec((TM_PROJ, HEAD_DIM), lambda i, j: (i, 0)),
            pl.BlockSpec((1, HEAD_DIM), lambda i, j: (0, 0)),
            pl.BlockSpec((1, HEAD_DIM), lambda i, j: (0, 0)),
        ],
        out_specs=pl.BlockSpec((TM_PROJ, TN_PROJ), lambda i, j: (i, j)),
        out_shape=jax.ShapeDtypeStruct((TOKENS, IN_WIDTH), BF16),
        scratch_shapes=[pltpu.VMEM((TM_PROJ, D_MODEL), BF16)],
        compiler_params=_params("parallel", "arbitrary"),
        name="in_proj",
    )(x2, mod3, mod3, norm1_g, w_in_bf, cos, sin, qg, kg)


_NT = (((1,), (1,)), ((), ()))


def _attn_kernel(q_ref, k_ref, v_ref, o_ref, kmean_scr):
    i = pl.program_id(2)
    scale = HEAD_DIM ** -0.5

    @pl.when(i == 0)
    def _():
        for n in range(N_KV_BLOCKS):
            kb = k_ref[n * MOBA_BLOCK:(n + 1) * MOBA_BLOCK, :].astype(F32)
            kmean_scr[n:n + 1, :] = jnp.mean(kb, axis=0, keepdims=True)

    q = q_ref[...]
    bs = lax.dot_general(q.astype(F32), kmean_scr[...], _NT, precision=HIGHEST,
                         preferred_element_type=F32)
    blk = lax.broadcasted_iota(jnp.int32, bs.shape, 1)
    past = blk < i
    bs = jnp.where(past, bs, NEG_INF)
    rank = jnp.zeros(bs.shape, F32)
    for m in range(N_KV_BLOCKS):
        col = bs[:, m:m + 1]
        tie = jnp.where(blk > m, 1.0, 0.0)
        rank = rank + jnp.where(col > bs, 1.0, jnp.where(col == bs, tie, 0.0))
    sel = jnp.where(past, jnp.where(rank < MOBA_TOPK, 1.0, 0.0), 0.0)

    row0 = pl.multiple_of(i * MOBA_BLOCK, MOBA_BLOCK)
    k_own = k_ref[pl.ds(row0, MOBA_BLOCK), :]
    v_own = v_ref[pl.ds(row0, MOBA_BLOCK), :]
    s = lax.dot_general(q, k_own, _NT, preferred_element_type=F32) * scale
    qi = lax.broadcasted_iota(jnp.int32, s.shape, 0)
    kj = lax.broadcasted_iota(jnp.int32, s.shape, 1)
    s = jnp.where(kj <= qi, s, NEG_INF)
    m0 = jnp.max(s, axis=-1, keepdims=True)
    p = jnp.exp(s - m0)
    l0 = jnp.sum(p, axis=-1, keepdims=True)
    acc0 = jnp.dot(p.astype(BF16), v_own, preferred_element_type=F32)

    def body(j, carry):
        m, l, acc = carry
        r = pl.multiple_of(j * MOBA_BLOCK, MOBA_BLOCK)
        kb = k_ref[pl.ds(r, MOBA_BLOCK), :]
        vb = v_ref[pl.ds(r, MOBA_BLOCK), :]
        sj = lax.dot_general(q, kb, _NT, preferred_element_type=F32) * scale
        chosen = jnp.sum(jnp.where(blk == j, sel, 0.0), axis=-1, keepdims=True)
        sj = jnp.where(chosen > 0.0, sj, NEG_INF)
        m_new = jnp.maximum(m, jnp.max(sj, axis=-1, keepdims=True))
        alpha = jnp.exp(m - m_new)
        pj = jnp.exp(sj - m_new)
        l_new = alpha * l + jnp.sum(pj, axis=-1, keepdims=True)
        acc_new = alpha * acc + jnp.dot(pj.astype(BF16), vb, preferred_element_type=F32)
        return m_new, l_new, acc_new

    _, l, acc = lax.fori_loop(0, i, body, (m0, l0, acc0))
    o_ref[...] = (acc / l).astype(BF16)


def _attention(proj):
    return pl.pallas_call(
        _attn_kernel,
        grid=(BATCH, N_HEADS, N_KV_BLOCKS),
        in_specs=[
            pl.BlockSpec((MOBA_BLOCK, HEAD_DIM), lambda b, h, i: (b * N_KV_BLOCKS + i, h)),
            pl.BlockSpec((SEQ, HEAD_DIM), lambda b, h, i: (b, COL_K + h)),
            pl.BlockSpec((SEQ, HEAD_DIM), lambda b, h, i: (b, COL_V + h)),
        ],
        out_specs=pl.BlockSpec((MOBA_BLOCK, HEAD_DIM), lambda b, h, i: (b * N_KV_BLOCKS + i, h)),
        out_shape=jax.ShapeDtypeStruct((TOKENS, ATTN_WIDTH), BF16),
        scratch_shapes=[pltpu.VMEM((N_KV_BLOCKS, HEAD_DIM), F32)],
        compiler_params=_params("parallel", "parallel", "arbitrary"),
        name="moba_attn",
    )(proj, proj, proj)


def _s5_prep_kernel(lr_r, li_r, ldt_r, lr_c, li_c, ldt_c, btr_ref, bti_ref, ctr_ref, cti_ref,
                    tm_ref, wst_ref, wout_ref, al_ref):
    L = S5_CHUNK
    ns = S5_TILE_STATES
    lr = lr_r[...]
    li = li_r[...]
    dt = jnp.exp(ldt_r[...])
    mag = jnp.exp(lr * dt)
    ab_re = mag * jnp.cos(li * dt)
    ab_im = mag * jnp.sin(li * dt)
    den = lr * lr + li * li
    nr = ab_re - 1.0
    ni = ab_im
    coef_re = (nr * lr + ni * li) / den
    coef_im = (ni * lr - nr * li) / den

    tau = jnp.minimum(lax.broadcasted_iota(jnp.int32, (4 * SUBLANES, ns), 0), L).astype(F32)
    pmag = jnp.exp(lr * dt * tau)
    pang = li * dt * tau
    p_re = pmag * jnp.cos(pang)
    p_im = pmag * jnp.sin(pang)
    w_re = coef_re * p_re - coef_im * p_im
    w_im = coef_re * p_im + coef_im * p_re

    btr = btr_ref[...]
    bti = bti_ref[...]
    ctr = ctr_ref[...]
    cti = cti_ref[...]

    tm_ref[...] = jnp.zeros(tm_ref.shape, tm_ref.dtype)
    for t in range(L):
        wr = w_re[t:t + 1, :]
        wi = w_im[t:t + 1, :]
        s_re = btr * wr - bti * wi
        s_im = btr * wi + bti * wr
        rp = L - 1 - t
        wst_ref[rp * LANES:(rp + 1) * LANES, 0:ns] = s_re.astype(BF16)
        wst_ref[rp * LANES:(rp + 1) * LANES, ns:2 * ns] = s_im.astype(BF16)
        kt = (jnp.dot(s_re, ctr, precision=HIGHEST, preferred_element_type=F32)
              - jnp.dot(s_im, cti, precision=HIGHEST, preferred_element_type=F32))
        ktb = kt.astype(BF16)
        for r0 in range(L - t):
            tm_ref[r0 * LANES:(r0 + 1) * LANES, (r0 + t) * LANES:(r0 + t + 1) * LANES] = ktb

    al_ref[:, 0:ns] = p_re[L:L + 1, :]
    al_ref[:, ns:2 * ns] = p_im[L:L + 1, :]

    lrc = lr_c[...]
    lic = li_c[...]
    dtc = jnp.exp(ldt_c[...])
    tauc = (jnp.minimum(lax.broadcasted_iota(jnp.int32, (ns, LANES), 1), L - 1) + 1).astype(F32)
    cmag = jnp.exp(lrc * dtc * tauc)
    cang = lic * dtc * tauc
    a_re = cmag * jnp.cos(cang)
    a_im = cmag * jnp.sin(cang)
    for r in range(L):
        ar = a_re[:, r:r + 1]
        ai = a_im[:, r:r + 1]
        wout_ref[0:ns, r * LANES:(r + 1) * LANES] = (ctr * ar - cti * ai).astype(BF16)
        wout_ref[ns:2 * ns, r * LANES:(r + 1) * LANES] = (-(ctr * ai + cti * ar)).astype(BF16)


def _s5_prep(lr_r, li_r, ldt_r, lr_c, li_c, ldt_c, btr, bti, ctr, cti):
    ns = S5_TILE_STATES
    row = pl.BlockSpec((None, 1, ns), lambda t: (t, 0, 0))
    col = pl.BlockSpec((None, ns, 1), lambda t: (t, 0, 0))
    bt = pl.BlockSpec((None, LANES, ns), lambda t: (t, 0, 0))
    ct = pl.BlockSpec((None, ns, LANES), lambda t: (t, 0, 0))
    return pl.pallas_call(
        _s5_prep_kernel,
        grid=(S5_TILES,),
        in_specs=[row, row, row, col, col, col, bt, bt, ct, ct],
        out_specs=[pl.BlockSpec((None, S5_FLAT, S5_FLAT), lambda t: (t, 0, 0)),
                   pl.BlockSpec((None, S5_FLAT, 2 * ns), lambda t: (t, 0, 0)),
                   pl.BlockSpec((None, 2 * ns, S5_FLAT), lambda t: (t, 0, 0)),
                   pl.BlockSpec((None, 1, 2 * ns), lambda t: (t, 0, 0))],
        out_shape=[jax.ShapeDtypeStruct((S5_TILES, S5_FLAT, S5_FLAT), BF16),
                   jax.ShapeDtypeStruct((S5_TILES, S5_FLAT, 2 * ns), BF16),
                   jax.ShapeDtypeStruct((S5_TILES, 2 * ns, S5_FLAT), BF16),
                   jax.ShapeDtypeStruct((S5_TILES, 1, 2 * ns), F32)],
        compiler_params=_params("parallel"),
        name="s5_prep",
    )(lr_r, li_r, ldt_r, lr_c, li_c, ldt_c, btr, bti, ctr, cti)


def _s5_kernel(u_ref, tm_ref, wst_ref, wout_ref, al_ref, d_ref, o_ref,
               uf_scr, uflat_scr, xs_scr, xp_scr, y_scr):
    L = S5_CHUNK
    ns = S5_TILE_STATES
    uf_scr[...] = u_ref[...].astype(F32)
    for r in range(L):
        uflat_scr[:, r * LANES:(r + 1) * LANES] = uf_scr[pl.ds(r, S5_CHUNKS, stride=L), :].astype(BF16)
    uflat = uflat_scr[...]

    xs_scr[...] = jnp.dot(uflat, wst_ref[...], preferred_element_type=F32)
    al = al_ref[...]
    al_re = al[:, 0:ns]
    al_im = al[:, ns:2 * ns]

    def body(c, state):
        s_re, s_im = state
        xp_scr[pl.ds(c, 1), 0:ns] = s_re
        xp_scr[pl.ds(c, 1), ns:2 * ns] = s_im
        x_re = xs_scr[pl.ds(c, 1), 0:ns]
        x_im = xs_scr[pl.ds(c, 1), ns:2 * ns]
        return (al_re * s_re - al_im * s_im + x_re, al_re * s_im + al_im * s_re + x_im)

    zero = jnp.zeros((1, ns), F32)
    lax.fori_loop(0, S5_CHUNKS, body, (zero, zero))

    y_scr[...] = (jnp.dot(uflat, tm_ref[...], preferred_element_type=F32)
                  + jnp.dot(xp_scr[...].astype(BF16), wout_ref[...], preferred_element_type=F32))
    d = d_ref[...]
    for r in range(L):
        ur = uf_scr[pl.ds(r, S5_CHUNKS, stride=L), :]
        yr = y_scr[:, r * LANES:(r + 1) * LANES] + d * ur
        uf_scr[pl.ds(r, S5_CHUNKS, stride=L), :] = jax.nn.gelu(yr, approximate=True)
    o_ref[...] = uf_scr[...].astype(BF16)


def _s5(proj, tmat, wst, wout, al, d_row):
    ns = S5_TILE_STATES
    return pl.pallas_call(
        _s5_kernel,
        grid=(S5_TILES, BATCH),
        in_specs=[
            pl.BlockSpec((SEQ, LANES), lambda t, b: (b, COL_U + t)),
            pl.BlockSpec((None, S5_FLAT, S5_FLAT), lambda t, b: (t, 0, 0)),
            pl.BlockSpec((None, S5_FLAT, 2 * ns), lambda t, b: (t, 0, 0)),
            pl.BlockSpec((None, 2 * ns, S5_FLAT), lambda t, b: (t, 0, 0)),
            pl.BlockSpec((None, 1, 2 * ns), lambda t, b: (t, 0, 0)),
            pl.BlockSpec((1, LANES), lambda t, b: (0, t)),
        ],
        out_specs=pl.BlockSpec((SEQ, LANES), lambda t, b: (b, t)),
        out_shape=jax.ShapeDtypeStruct((TOKENS, SSM_WIDTH), BF16),
        scratch_shapes=[pltpu.VMEM((SEQ, LANES), F32),
                        pltpu.VMEM((S5_CHUNKS, S5_FLAT), BF16),
                        pltpu.VMEM((S5_CHUNKS, 2 * ns), F32),
                        pltpu.VMEM((S5_CHUNKS, 2 * ns), F32),
                        pltpu.VMEM((S5_CHUNKS, S5_FLAT), F32)],
        compiler_params=_params("parallel", "parallel"),
        name="s5_scan",
    )(proj, tmat, wst, wout, al, d_row)


def _merge_kernel(x_ref, gl_ref, ya_ref, ga_ref, gs_ref, g1_ref, wglu_ref, bglu_ref,
                  wa_ref, ws_ref, wo_ref, o_ref):
    gl = gl_ref[...]
    z = jnp.dot(gl, wglu_ref[...], preferred_element_type=F32) + bglu_ref[...]
    y_ssm = (gl.astype(F32) * jax.nn.sigmoid(z)).astype(BF16)
    a = jnp.dot(ya_ref[...], wa_ref[...], preferred_element_type=F32)
    s = jnp.dot(y_ssm, ws_ref[...], preferred_element_type=F32)
    merged = (jax.nn.sigmoid(ga_ref[...].astype(F32)) * a
              + jax.nn.sigmoid(gs_ref[...].astype(F32)) * s)
    out = jnp.dot(merged.astype(BF16), wo_ref[...], preferred_element_type=F32)
    o_ref[...] = x_ref[...] + g1_ref[...] * out


def _merge(x2, gl, y_attn, proj, mod3, w_glu, b_glu, w_attn_br, w_ssm_br, w_out):
    tm = TM_MERGE
    tiles_per_seq = SEQ // tm
    gate_blk = (3 * ATTN_WIDTH + SSM_WIDTH) // D_MODEL
    const = lambda shape: pl.BlockSpec(shape, lambda i: (0, 0), pipeline_mode=pl.Buffered(1))
    return pl.pallas_call(
        _merge_kernel,
        grid=(TOKENS // tm,),
        in_specs=[
            pl.BlockSpec((tm, D_MODEL), lambda i: (i, 0)),
            pl.BlockSpec((tm, SSM_WIDTH), lambda i: (i, 0)),
            pl.BlockSpec((tm, ATTN_WIDTH), lambda i: (i, 0)),
            pl.BlockSpec((tm, D_MODEL), lambda i: (i, gate_blk)),
            pl.BlockSpec((tm, D_MODEL), lambda i: (i, gate_blk + 1)),
            pl.BlockSpec((None, 1, D_MODEL), lambda i: (i // tiles_per_seq * N_MOD + 2, 0, 0)),
            const((SSM_WIDTH, SSM_WIDTH)),
            const((1, SSM_WIDTH)),
            const((ATTN_WIDTH, D_MODEL)),
            const((SSM_WIDTH, D_MODEL)),
            const((D_MODEL, D_MODEL)),
        ],
        out_specs=pl.BlockSpec((tm, D_MODEL), lambda i: (i, 0)),
        out_shape=jax.ShapeDtypeStruct((TOKENS, D_MODEL), F32),
        compiler_params=_params("parallel"),
        name="merge",
    )(x2, gl, y_attn, proj, proj, mod3, w_glu, b_glu, w_attn_br, w_ssm_br, w_out)


def _ffn_kernel(x_ref, halo_ref, sc_ref, sh_ref, g2_ref, ng_ref, wv_ref, wg_ref,
                cwv_ref, cwg_ref, cbv_ref, cbg_ref, wd_ref, o_ref, h_scr, acc_scr):
    i = pl.program_id(0)
    j = pl.program_id(1)
    tm = TM_FFN

    @pl.when(j == 0)
    def _():
        g = ng_ref[...]
        sc = sc_ref[...]
        sh = sh_ref[...]
        h_scr[FFN_HALO:, :] = _rms_modulate(x_ref[...], g, sc, sh).astype(BF16)
        halo = _rms_modulate(halo_ref[...], g, sc, sh)
        seq_start = (i % (SEQ // tm)) == 0
        h_scr[0:FFN_HALO, :] = jnp.where(seq_start, 0.0, halo).astype(BF16)
        acc_scr[...] = jnp.zeros(acc_scr.shape, acc_scr.dtype)

    h = h_scr[...]

    def conv(w_ref, cw_ref, cb_ref):
        up = jnp.dot(h, w_ref[...], preferred_element_type=F32)
        cw = cw_ref[...]
        up1 = pltpu.roll(up, 1, axis=0)
        up2 = pltpu.roll(up, 2, axis=0)
        y = cw[2:3, :] * up + cw[1:2, :] * up1 + cw[0:1, :] * up2 + cb_ref[...]
        return y[FFN_HALO:, :]

    val = conv(wv_ref, cwv_ref, cbv_ref)
    gt = conv(wg_ref, cwg_ref, cbg_ref)
    act = (gt * jax.nn.sigmoid(gt) * val).astype(BF16)
    acc_scr[...] += jnp.dot(act, wd_ref[...], preferred_element_type=F32)

    @pl.when(j == N_F_TILES - 1)
    def _():
        o_ref[...] = x_ref[...] + g2_ref[...] * acc_scr[...]


def _ffn(x1, mod3, norm2_g, w_up, conv_w, conv_b, w_down):
    tm, tf = TM_FFN, TF_FFN
    tiles_per_seq = SEQ // tm
    halo_per_tile = tm // FFN_HALO
    modspec = lambda k: pl.BlockSpec((None, 1, D_MODEL),
                                     lambda i, j: (i // tiles_per_seq * N_MOD + k, 0, 0))
    return pl.pallas_call(
        _ffn_kernel,
        grid=(TOKENS // tm, N_F_TILES),
        in_specs=[
            pl.BlockSpec((tm, D_MODEL), lambda i, j: (i, 0)),
            pl.BlockSpec((FFN_HALO, D_MODEL), lambda i, j: (jnp.maximum(i * halo_per_tile - 1, 0), 0)),
            modspec(4), modspec(3), modspec(5),
            pl.BlockSpec((1, D_MODEL), lambda i, j: (0, 0)),
            pl.BlockSpec((D_MODEL, tf), lambda i, j: (0, j)),
            pl.BlockSpec((D_MODEL, tf), lambda i, j: (0, N_F_TILES + j)),
            pl.BlockSpec((3, tf), lambda i, j: (0, j)),
            pl.BlockSpec((3, tf), lambda i, j: (0, N_F_TILES + j)),
            pl.BlockSpec((1, tf), lambda i, j: (0, j)),
            pl.BlockSpec((1, tf), lambda i, j: (0, N_F_TILES + j)),
            pl.BlockSpec((tf, D_MODEL), lambda i, j: (j, 0)),
        ],
        out_specs=pl.BlockSpec((tm, D_MODEL), lambda i, j: (i, 0)),
        out_shape=jax.ShapeDtypeStruct((TOKENS, D_MODEL), F32),
        scratch_shapes=[pltpu.VMEM((FFN_HALO + tm, D_MODEL), BF16),
                        pltpu.VMEM((tm, D_MODEL), F32)],
        compiler_params=_params("parallel", "arbitrary"),
        name="ffn",
    )(x1, x1, mod3, mod3, mod3, norm2_g, w_up, w_up, conv_w, conv_w, conv_b, conv_b, w_down)


def _block_diag(w):
    _, r, c = w.shape
    g = S5_TILE_GROUPS
    w = w.reshape(S5_TILES, g, r, 1, c)
    eye = jnp.eye(g, dtype=bool).reshape(1, g, 1, g, 1)
    return jnp.where(eye, w, 0.0).reshape(S5_TILES, g * r, g * c)


def _layer(x2, c, pos_col, w_mod, b_mod, norm1_g, w_in, q_norm_g, k_norm_g,
           lam_re, lam_im, log_dt, b_re, b_im, c_re, c_im, d_skip, w_glu, b_glu,
           w_attn_br, w_ssm_br, w_out, norm2_g, w_up, conv_w, conv_b, w_down):
    c_pad = jnp.pad(c, ((0, SUBLANES - BATCH), (0, 0)))
    mod = _mod(c_pad, w_mod, b_mod.reshape(1, -1))[:BATCH]
    mod3 = mod.reshape(BATCH * N_MOD, 1, D_MODEL)

    half = HEAD_DIM // 2
    inv_freq = ROPE_THETA ** (-jnp.arange(half, dtype=F32) / half)
    freq_row = jnp.concatenate([inv_freq, inv_freq]).reshape(1, HEAD_DIM)
    cos, sin = _rope_tables(pos_col, freq_row)

    proj = _in_proj(x2, mod3, norm1_g.reshape(1, -1), w_in.astype(BF16), cos, sin,
                    q_norm_g.reshape(1, -1), k_norm_g.reshape(1, -1))
    y_attn = _attention(proj)

    ns = S5_TILE_STATES
    ldt = jnp.repeat(log_dt, SSM_STATE)
    rows = [a.reshape(S5_TILES, 1, ns) for a in (lam_re, lam_im, ldt)]
    cols = [a.reshape(S5_TILES, ns, 1) for a in (lam_re, lam_im, ldt)]
    btr = _block_diag(b_re.transpose(0, 2, 1))
    bti = _block_diag(b_im.transpose(0, 2, 1))
    ctr = _block_diag(c_re.transpose(0, 2, 1))
    cti = _block_diag(c_im.transpose(0, 2, 1))
    tmat, wst, wout, al = _s5_prep(*rows, *cols, btr, bti, ctr, cti)
    gl = _s5(proj, tmat, wst, wout, al, d_skip.reshape(1, -1))

    x1 = _merge(x2, gl, y_attn, proj, mod3, w_glu.astype(BF16), b_glu.reshape(1, -1),
                w_attn_br.astype(BF16), w_ssm_br.astype(BF16), w_out.astype(BF16))
    return _ffn(x1, mod3, norm2_g.reshape(1, -1), w_up.astype(BF16), conv_w,
                conv_b.reshape(1, -1), w_down.astype(BF16))


def kernel(x, c, positions, w_mod, b_mod, norm1_g, w_in, q_norm_g, k_norm_g, ssm_lambda_re, ssm_lambda_im, ssm_log_dt, ssm_b_re, ssm_b_im, ssm_c_re, ssm_c_im, ssm_d, w_glu, b_glu, w_attn_br, w_ssm_br, w_out, norm2_g, w_up, conv_w, conv_b, w_down):
    assert x.shape == (BATCH, SEQ, D_MODEL) and w_in.shape[0] == 1
    x2 = x.reshape(TOKENS, D_MODEL)
    pos_col = positions.reshape(TOKENS, 1)
    for l in range(w_in.shape[0]):
        x2 = _layer(x2, c, pos_col, w_mod[l], b_mod[l], norm1_g[l], w_in[l], q_norm_g[l], k_norm_g[l],
                    ssm_lambda_re[l], ssm_lambda_im[l], ssm_log_dt[l], ssm_b_re[l], ssm_b_im[l],
                    ssm_c_re[l], ssm_c_im[l], ssm_d[l], w_glu[l], b_glu[l], w_attn_br[l],
                    w_ssm_br[l], w_out[l], norm2_g[l], w_up[l], conv_w[l], conv_b[l], w_down[l])
    return x2.reshape(BATCH, SEQ, D_MODEL)
```

```python
import functools
import math

import jax
import jax.numpy as jnp
from jax import lax
from jax.experimental import pallas as pl
from jax.experimental.pallas import tpu as pltpu

F32 = jnp.float32
BF16 = jnp.bfloat16
HIGHEST = lax.Precision.HIGHEST

D_MODEL = 2048
BATCH = 4
SEQ = 4096
TOKENS = BATCH * SEQ
HEAD_DIM = 128
N_HEADS = 8
ATTN_WIDTH = N_HEADS * HEAD_DIM
MOBA_BLOCK = 256
N_KV_BLOCKS = SEQ // MOBA_BLOCK
MOBA_TOPK = 3
ROPE_THETA = 10000.0
SSM_WIDTH = D_MODEL // 2
SSM_GROUP = 16
SSM_GROUPS = SSM_WIDTH // SSM_GROUP
SSM_STATE = 64
FFN_HIDDEN = 5632
N_MOD = 6
IN_WIDTH = 3 * ATTN_WIDTH + SSM_WIDTH + 2 * D_MODEL
EPS = 1e-6
NEG_INF = -1e30

LANES = 128
SUBLANES = 8
VMEM_LIMIT = 56 * 1024 * 1024

TM_PROJ = 512
TN_PROJ = 1024
ATTN_GROUP = 4
TM_MERGE = 256
TM_FFN = 512
TF_FFN = 512
N_F_TILES = FFN_HIDDEN // TF_FFN
FFN_HALO = 16
S5_TILE_GROUPS = LANES // SSM_GROUP
S5_TILES = SSM_GROUPS // S5_TILE_GROUPS
S5_TILE_STATES = S5_TILE_GROUPS * SSM_STATE
S5_CHUNK = 16
S5_CHUNKS = SEQ // S5_CHUNK
S5_FLAT = S5_CHUNK * LANES

COL_K = ATTN_WIDTH // LANES
COL_V = 2 * ATTN_WIDTH // LANES
COL_U = 3 * ATTN_WIDTH // LANES


def _params(*sem):
    return pltpu.CompilerParams(dimension_semantics=sem, vmem_limit_bytes=VMEM_LIMIT)


def _mod_kernel(c_ref, w_ref, b_ref, o_ref):
    c = c_ref[...]
    sc = c * jax.nn.sigmoid(c)
    o_ref[...] = jnp.dot(sc, w_ref[...], precision=HIGHEST, preferred_element_type=F32) + b_ref[...]


def _mod(c_pad, w_mod, b_mod):
    tn = 1024
    n = N_MOD * D_MODEL
    return pl.pallas_call(
        _mod_kernel,
        grid=(n // tn,),
        in_specs=[pl.BlockSpec((SUBLANES, D_MODEL), lambda j: (0, 0)),
                  pl.BlockSpec((D_MODEL, tn), lambda j: (0, j)),
                  pl.BlockSpec((1, tn), lambda j: (0, j))],
        out_specs=pl.BlockSpec((SUBLANES, tn), lambda j: (0, j)),
        out_shape=jax.ShapeDtypeStruct((SUBLANES, n), F32),
        compiler_params=_params("parallel"),
        name="mod",
    )(c_pad, w_mod, b_mod)


def _rope_kernel(pos_ref, freq_ref, cos_ref, sin_ref):
    ang = pos_ref[...].astype(F32) * freq_ref[...]
    lane = lax.broadcasted_iota(jnp.int32, ang.shape, 1)
    cos_ref[...] = jnp.cos(ang)
    sin_ref[...] = jnp.where(lane < HEAD_DIM // 2, -1.0, 1.0) * jnp.sin(ang)


def _rope_tables(pos_col, freq_row):
    tr = 2048
    return pl.pallas_call(
        _rope_kernel,
        grid=(TOKENS // tr,),
        in_specs=[pl.BlockSpec((tr, 1), lambda i: (i, 0)),
                  pl.BlockSpec((1, HEAD_DIM), lambda i: (0, 0))],
        out_specs=[pl.BlockSpec((tr, HEAD_DIM), lambda i: (i, 0)),
                   pl.BlockSpec((tr, HEAD_DIM), lambda i: (i, 0))],
        out_shape=[jax.ShapeDtypeStruct((TOKENS, HEAD_DIM), F32)] * 2,
        compiler_params=_params("parallel"),
        name="rope",
    )(pos_col, freq_row)


def _rms_modulate(x, g, scale, shift):
    ms = jnp.mean(x * x, axis=-1, keepdims=True)
    y = x * lax.rsqrt(ms + EPS) * g
    return y * (1.0 + scale) + shift


def _in_proj_kernel(x_ref, sc_ref, sh_ref, g_ref, w_ref, cos_ref, sin_ref, qg_ref, kg_ref,
                    o_ref, h_scr):
    j = pl.program_id(1)

    @pl.when(j == 0)
    def _():
        h_scr[...] = _rms_modulate(x_ref[...], g_ref[...], sc_ref[...], sh_ref[...]).astype(BF16)

    acc = jnp.dot(h_scr[...], w_ref[...], preferred_element_type=F32)

    @pl.when(j < 2)
    def _():
        g = jnp.where(j == 0, qg_ref[...], kg_ref[...])
        out_scale = jnp.where(j == 0, HEAD_DIM ** -0.5, 1.0)
        cos = cos_ref[...] * out_scale
        sin = sin_ref[...] * out_scale
        for hd in range(N_HEADS):
            blk = acc[:, hd * HEAD_DIM:(hd + 1) * HEAD_DIM]
            ms = jnp.mean(blk * blk, axis=-1, keepdims=True)
            y = blk * lax.rsqrt(ms + EPS) * g
            rot = y * cos + pltpu.roll(y, HEAD_DIM // 2, axis=1) * sin
            o_ref[:, hd * HEAD_DIM:(hd + 1) * HEAD_DIM] = rot.astype(BF16)

    @pl.when(j >= 2)
    def _():
        o_ref[...] = acc.astype(BF16)


def _in_proj(x2, mod3, norm1_g, w_in_bf, cos, sin, qg, kg):
    tiles_per_seq = SEQ // TM_PROJ
    return pl.pallas_call(
        _in_proj_kernel,
        grid=(TOKENS // TM_PROJ, IN_WIDTH // TN_PROJ),
        in_specs=[
            pl.BlockSpec((TM_PROJ, D_MODEL), lambda i, j: (i, 0)),
            pl.BlockSpec((None, 1, D_MODEL), lambda i, j: (i // tiles_per_seq * N_MOD + 1, 0, 0)),
            pl.BlockSpec((None, 1, D_MODEL), lambda i, j: (i // tiles_per_seq * N_MOD + 0, 0, 0)),
            pl.BlockSpec((1, D_MODEL), lambda i, j: (0, 0)),
            pl.BlockSpec((D_MODEL, TN_PROJ), lambda i, j: (0, j)),
            pl.BlockSpec((TM_PROJ, HEAD_DIM), lambda i, j: (i, 0)),
            pl.BlockSpec((TM_PROJ, HEAD_DIM), lambda i, j: (i, 0)),
            pl.BlockSpec((1, HEAD_DIM), lambda i, j: (0, 0)),
            pl.BlockSpec((1, HEAD_DIM), lambda i, j: (0, 0)),
        ],
        out_specs=pl.BlockSpec((TM_PROJ, TN_PROJ), lambda i, j: (i, j)),
        out_shape=jax.ShapeDtypeStruct((TOKENS, IN_WIDTH), BF16),
        scratch_shapes=[pltpu.VMEM((TM_PROJ, D_MODEL), BF16)],
        compiler_params=_params("parallel", "arbitrary"),
        name="in_proj",
    )(x2, mod3, mod3, norm1_g, w_in_bf, cos, sin, qg, kg)


_NT = (((1,), (1,)), ((), ()))


def _attn_kernel(q_ref, k_ref, v_ref, o_ref, kmean_scr, vt_scr, s_scr):
    i = pl.program_id(2)

    @pl.when(i == 0)
    def _():
        for n in range(N_KV_BLOCKS):
            rows = slice(n * MOBA_BLOCK, (n + 1) * MOBA_BLOCK)
            kmean_scr[n:n + 1, :] = jnp.mean(k_ref[rows, :].astype(F32), axis=0, keepdims=True)
            vt_scr[n] = v_ref[rows, :].astype(F32).T.astype(BF16)

    q = q_ref[...]
    bs = lax.dot_general(kmean_scr[...], q.astype(F32), _NT, precision=HIGHEST,
                         preferred_element_type=F32)
    blk = lax.broadcasted_iota(jnp.int32, bs.shape, 0)
    past = blk < i
    bs = jnp.where(past, bs, NEG_INF)
    rank = jnp.zeros(bs.shape, F32)
    for m in range(N_KV_BLOCKS):
        row = bs[m:m + 1, :]
        tie = jnp.where(blk > m, 1.0, 0.0)
        rank = rank + jnp.where(row > bs, 1.0, jnp.where(row == bs, tie, 0.0))
    bias = jnp.where(past, jnp.where(rank < MOBA_TOPK, 0.0, NEG_INF), NEG_INF)

    def fold8(t):
        return t.reshape(MOBA_BLOCK // SUBLANES, SUBLANES, MOBA_BLOCK)

    own = slice(N_KV_BLOCKS * MOBA_BLOCK, (N_KV_BLOCKS + 1) * MOBA_BLOCK)
    row0 = pl.multiple_of(i * MOBA_BLOCK, MOBA_BLOCK)
    s = lax.dot_general(k_ref[pl.ds(row0, MOBA_BLOCK), :], q, _NT, preferred_element_type=F32)
    ki = lax.broadcasted_iota(jnp.int32, s.shape, 0)
    qi = lax.broadcasted_iota(jnp.int32, s.shape, 1)
    s = jnp.where(ki <= qi, s, NEG_INF)
    s_scr[own, :] = s
    m_own = jnp.max(fold8(s), axis=0)

    def attend(n_blocks):
        m8 = m_own
        for n in range(n_blocks):
            rows = slice(n * MOBA_BLOCK, (n + 1) * MOBA_BLOCK)
            sn = lax.dot_general(k_ref[rows, :], q, _NT, preferred_element_type=F32) + bias[n:n + 1, :]
            s_scr[rows, :] = sn
            m8 = jnp.maximum(m8, jnp.max(fold8(sn), axis=0))
        m = jnp.max(m8, axis=0, keepdims=True)
        p = jnp.exp(s_scr[own, :] - m)
        l8 = jnp.sum(fold8(p), axis=0)
        acc = jnp.dot(vt_scr[i], p.astype(BF16), preferred_element_type=F32)
        for n in range(n_blocks):
            rows = slice(n * MOBA_BLOCK, (n + 1) * MOBA_BLOCK)
            p = jnp.exp(s_scr[rows, :] - m)
            l8 = l8 + jnp.sum(fold8(p), axis=0)
            acc = acc + jnp.dot(vt_scr[n], p.astype(BF16), preferred_element_type=F32)
        l = jnp.sum(l8, axis=0, keepdims=True)
        o_ref[...] = (acc / l).T.astype(BF16)

    for n_blocks in range(0, N_KV_BLOCKS + 1, ATTN_GROUP):
        pl.when((i + ATTN_GROUP - 1) // ATTN_GROUP == n_blocks // ATTN_GROUP)(
            functools.partial(attend, n_blocks))


def _attention(proj):
    return pl.pallas_call(
        _attn_kernel,
        grid=(BATCH, N_HEADS, N_KV_BLOCKS),
        in_specs=[
            pl.BlockSpec((MOBA_BLOCK, HEAD_DIM), lambda b, h, i: (b * N_KV_BLOCKS + i, h)),
            pl.BlockSpec((SEQ, HEAD_DIM), lambda b, h, i: (b, COL_K + h)),
            pl.BlockSpec((SEQ, HEAD_DIM), lambda b, h, i: (b, COL_V + h)),
        ],
        out_specs=pl.BlockSpec((MOBA_BLOCK, HEAD_DIM), lambda b, h, i: (b * N_KV_BLOCKS + i, h)),
        out_shape=jax.ShapeDtypeStruct((TOKENS, ATTN_WIDTH), BF16),
        scratch_shapes=[pltpu.VMEM((N_KV_BLOCKS, HEAD_DIM), F32),
                        pltpu.VMEM((N_KV_BLOCKS, HEAD_DIM, MOBA_BLOCK), BF16),
                        pltpu.VMEM(((N_KV_BLOCKS + 1) * MOBA_BLOCK, MOBA_BLOCK), F32)],
        compiler_params=_params("parallel", "parallel", "arbitrary"),
        name="moba_attn",
    )(proj, proj, proj)


def _s5_prep_kernel(lr_r, li_r, ldt_r, lr_c, li_c, ldt_c, btr_ref, bti_ref, ctr_ref, cti_ref,
                    tm_ref, wst_ref, wout_ref, al_ref):
    L = S5_CHUNK
    ns = S5_TILE_STATES
    lr = lr_r[...]
    li = li_r[...]
    dt = jnp.exp(ldt_r[...])
    mag = jnp.exp(lr * dt)
    ab_re = mag * jnp.cos(li * dt)
    ab_im = mag * jnp.sin(li * dt)
    den = lr * lr + li * li
    nr = ab_re - 1.0
    ni = ab_im
    coef_re = (nr * lr + ni * li) / den
    coef_im = (ni * lr - nr * li) / den

    tau = jnp.minimum(lax.broadcasted_iota(jnp.int32, (4 * SUBLANES, ns), 0), L).astype(F32)
    pmag = jnp.exp(lr * dt * tau)
    pang = li * dt * tau
    p_re = pmag * jnp.cos(pang)
    p_im = pmag * jnp.sin(pang)
    w_re = coef_re * p_re - coef_im * p_im
    w_im = coef_re * p_im + coef_im * p_re

    btr = btr_ref[...]
    bti = bti_ref[...]
    ctr = ctr_ref[...]
    cti = cti_ref[...]

    tm_ref[...] = jnp.zeros(tm_ref.shape, tm_ref.dtype)
    for t in range(L):
        wr = w_re[t:t + 1, :]
        wi = w_im[t:t + 1, :]
        s_re = btr * wr - bti * wi
        s_im = btr * wi + bti * wr
        rp = L - 1 - t
        wst_ref[rp * LANES:(rp + 1) * LANES, 0:ns] = s_re.astype(BF16)
        wst_ref[rp * LANES:(rp + 1) * LANES, ns:2 * ns] = s_im.astype(BF16)
        kt = (jnp.dot(s_re, ctr, precision=HIGHEST, preferred_element_type=F32)
              - jnp.dot(s_im, cti, precision=HIGHEST, preferred_element_type=F32))
        ktb = kt.astype(BF16)
        for r0 in range(L - t):
            tm_ref[r0 * LANES:(r0 + 1) * LANES, (r0 + t) * LANES:(r0 + t + 1) * LANES] = ktb

    al_ref[:, 0:ns] = p_re[L:L + 1, :]
    al_ref[:, ns:2 * ns] = p_im[L:L + 1, :]

    lrc = lr_c[...]
    lic = li_c[...]
    dtc = jnp.exp(ldt_c[...])
    tauc = (jnp.minimum(lax.broadcasted_iota(jnp.int32, (ns, LANES), 1), L - 1) + 1).astype(F32)
    cmag = jnp.exp(lrc * dtc * tauc)
    cang = lic * dtc * tauc
    a_re = cmag * jnp.cos(cang)
    a_im = cmag * jnp.sin(cang)
    for r in range(L):
        ar = a_re[:, r:r + 1]
        ai = a_im[:, r:r + 1]
        wout_ref[0:ns, r * LANES:(r + 1) * LANES] = (ctr * ar - cti * ai).astype(BF16)
        wout_ref[ns:2 * ns, r * LANES:(r + 1) * LANES] = (-(ctr * ai + cti * ar)).astype(BF16)


def _s5_prep(lr_r, li_r, ldt_r, lr_c, li_c, ldt_c, btr, bti, ctr, cti):
    ns = S5_TILE_STATES
    row = pl.BlockSpec((None, 1, ns), lambda t: (t, 0, 0))
    col = pl.BlockSpec((None, ns, 1), lambda t: (t, 0, 0))
    bt = pl.BlockSpec((None, LANES, ns), lambda t: (t, 0, 0))
    ct = pl.BlockSpec((None, ns, LANES), lambda t: (t, 0, 0))
    return pl.pallas_call(
        _s5_prep_kernel,
        grid=(S5_TILES,),
        in_specs=[row, row, row, col, col, col, bt, bt, ct, ct],
        out_specs=[pl.BlockSpec((None, S5_FLAT, S5_FLAT), lambda t: (t, 0, 0)),
                   pl.BlockSpec((None, S5_FLAT, 2 * ns), lambda t: (t, 0, 0)),
                   pl.BlockSpec((None, 2 * ns, S5_FLAT), lambda t: (t, 0, 0)),
                   pl.BlockSpec((None, 1, 2 * ns), lambda t: (t, 0, 0))],
        out_shape=[jax.ShapeDtypeStruct((S5_TILES, S5_FLAT, S5_FLAT), BF16),
                   jax.ShapeDtypeStruct((S5_TILES, S5_FLAT, 2 * ns), BF16),
                   jax.ShapeDtypeStruct((S5_TILES, 2 * ns, S5_FLAT), BF16),
                   jax.ShapeDtypeStruct((S5_TILES, 1, 2 * ns), F32)],
        compiler_params=_params("parallel"),
        name="s5_prep",
    )(lr_r, li_r, ldt_r, lr_c, li_c, ldt_c, btr, bti, ctr, cti)


def _s5_kernel(u_ref, tm_ref, wst_ref, wout_ref, al_ref, d_ref, o_ref,
               uf_scr, uflat_scr, xs_scr, xp_scr, y_scr):
    L = S5_CHUNK
    ns = S5_TILE_STATES
    uf_scr[...] = u_ref[...].astype(F32)
    for r in range(L):
        uflat_scr[:, r * LANES:(r + 1) * LANES] = uf_scr[pl.ds(r, S5_CHUNKS, stride=L), :].astype(BF16)
    uflat = uflat_scr[...]

    xs_scr[...] = jnp.dot(uflat, wst_ref[...], preferred_element_type=F32)
    al = al_ref[...]
    al_re = al[:, 0:ns]
    al_im = al[:, ns:2 * ns]

    def body(c, state):
        s_re, s_im = state
        xp_scr[pl.ds(c, 1), 0:ns] = s_re
        xp_scr[pl.ds(c, 1), ns:2 * ns] = s_im
        x_re = xs_scr[pl.ds(c, 1), 0:ns]
        x_im = xs_scr[pl.ds(c, 1), ns:2 * ns]
        return (al_re * s_re - al_im * s_im + x_re, al_re * s_im + al_im * s_re + x_im)

    zero = jnp.zeros((1, ns), F32)
    lax.fori_loop(0, S5_CHUNKS, body, (zero, zero))

    y_scr[...] = (jnp.dot(uflat, tm_ref[...], preferred_element_type=F32)
                  + jnp.dot(xp_scr[...].astype(BF16), wout_ref[...], preferred_element_type=F32))
    d = d_ref[...]
    for r in range(L):
        ur = uf_scr[pl.ds(r, S5_CHUNKS, stride=L), :]
        yr = y_scr[:, r * LANES:(r + 1) * LANES] + d * ur
        uf_scr[pl.ds(r, S5_CHUNKS, stride=L), :] = jax.nn.gelu(yr, approximate=True)
    o_ref[...] = uf_scr[...].astype(BF16)


def _s5(proj, tmat, wst, wout, al, d_row):
    ns = S5_TILE_STATES
    return pl.pallas_call(
        _s5_kernel,
        grid=(S5_TILES, BATCH),
        in_specs=[
            pl.BlockSpec((SEQ, LANES), lambda t, b: (b, COL_U + t)),
            pl.BlockSpec((None, S5_FLAT, S5_FLAT), lambda t, b: (t, 0, 0)),
            pl.BlockSpec((None, S5_FLAT, 2 * ns), lambda t, b: (t, 0, 0)),
            pl.BlockSpec((None, 2 * ns, S5_FLAT), lambda t, b: (t, 0, 0)),
            pl.BlockSpec((None, 1, 2 * ns), lambda t, b: (t, 0, 0)),
            pl.BlockSpec((1, LANES), lambda t, b: (0, t)),
        ],
        out_specs=pl.BlockSpec((SEQ, LANES), lambda t, b: (b, t)),
        out_shape=jax.ShapeDtypeStruct((TOKENS, SSM_WIDTH), BF16),
        scratch_shapes=[pltpu.VMEM((SEQ, LANES), F32),
                        pltpu.VMEM((S5_CHUNKS, S5_FLAT), BF16),
                        pltpu.VMEM((S5_CHUNKS, 2 * ns), F32),
                        pltpu.VMEM((S5_CHUNKS, 2 * ns), F32),
                        pltpu.VMEM((S5_CHUNKS, S5_FLAT), F32)],
        compiler_params=_params("parallel", "parallel"),
        name="s5_scan",
    )(proj, tmat, wst, wout, al, d_row)


def _merge_kernel(x_ref, gl_ref, ya_ref, ga_ref, gs_ref, g1_ref, wglu_ref, bglu_ref,
                  wa_ref, ws_ref, wo_ref, o_ref):
    gl = gl_ref[...]
    z = jnp.dot(gl, wglu_ref[...], preferred_element_type=F32) + bglu_ref[...]
    y_ssm = (gl.astype(F32) * jax.nn.sigmoid(z)).astype(BF16)
    a = jnp.dot(ya_ref[...], wa_ref[...], preferred_element_type=F32)
    s = jnp.dot(y_ssm, ws_ref[...], preferred_element_type=F32)
    merged = (jax.nn.sigmoid(ga_ref[...].astype(F32)) * a
              + jax.nn.sigmoid(gs_ref[...].astype(F32)) * s)
    out = jnp.dot(merged.astype(BF16), wo_ref[...], preferred_element_type=F32)
    o_ref[...] = x_ref[...] + g1_ref[...] * out


def _merge(x2, gl, y_attn, proj, mod3, w_glu, b_glu, w_attn_br, w_ssm_br, w_out):
    tm = TM_MERGE
    tiles_per_seq = SEQ // tm
    gate_blk = (3 * ATTN_WIDTH + SSM_WIDTH) // D_MODEL
    const = lambda shape: pl.BlockSpec(shape, lambda i: (0, 0), pipeline_mode=pl.Buffered(1))
    return pl.pallas_call(
        _merge_kernel,
        grid=(TOKENS // tm,),
        in_specs=[
            pl.BlockSpec((tm, D_MODEL), lambda i: (i, 0)),
            pl.BlockSpec((tm, SSM_WIDTH), lambda i: (i, 0)),
            pl.BlockSpec((tm, ATTN_WIDTH), lambda i: (i, 0)),
            pl.BlockSpec((tm, D_MODEL), lambda i: (i, gate_blk)),
            pl.BlockSpec((tm, D_MODEL), lambda i: (i, gate_blk + 1)),
            pl.BlockSpec((None, 1, D_MODEL), lambda i: (i // tiles_per_seq * N_MOD + 2, 0, 0)),
            const((SSM_WIDTH, SSM_WIDTH)),
            const((1, SSM_WIDTH)),
            const((ATTN_WIDTH, D_MODEL)),
            const((SSM_WIDTH, D_MODEL)),
            const((D_MODEL, D_MODEL)),
        ],
        out_specs=pl.BlockSpec((tm, D_MODEL), lambda i: (i, 0)),
        out_shape=jax.ShapeDtypeStruct((TOKENS, D_MODEL), F32),
        compiler_params=_params("parallel"),
        name="merge",
    )(x2, gl, y_attn, proj, proj, mod3, w_glu, b_glu, w_attn_br, w_ssm_br, w_out)


def _ffn_kernel(x_ref, halo_ref, sc_ref, sh_ref, g2_ref, ng_ref, wv_ref, wg_ref,
                cwv_ref, cwg_ref, cbv_ref, cbg_ref, wd_ref, o_ref, h_scr, acc_scr):
    i = pl.program_id(0)
    j = pl.program_id(1)
    tm = TM_FFN

    @pl.when(j == 0)
    def _():
        g = ng_ref[...]
        sc = sc_ref[...]
        sh = sh_ref[...]
        h_scr[FFN_HALO:, :] = _rms_modulate(x_ref[...], g, sc, sh).astype(BF16)
        halo = _rms_modulate(halo_ref[...], g, sc, sh)
        seq_start = (i % (SEQ // tm)) == 0
        h_scr[0:FFN_HALO, :] = jnp.where(seq_start, 0.0, halo).astype(BF16)
        acc_scr[...] = jnp.zeros(acc_scr.shape, acc_scr.dtype)

    h = h_scr[...]

    def conv(w_ref, cw_ref, cb_ref):
        up = jnp.dot(h, w_ref[...], preferred_element_type=F32)
        cw = cw_ref[...]
        up1 = pltpu.roll(up, 1, axis=0)
        up2 = pltpu.roll(up, 2, axis=0)
        y = cw[2:3, :] * up + cw[1:2, :] * up1 + cw[0:1, :] * up2 + cb_ref[...]
        return y[FFN_HALO:, :]

    val = conv(wv_ref, cwv_ref, cbv_ref)
    gt = conv(wg_ref, cwg_ref, cbg_ref)
    act = (gt * jax.nn.sigmoid(gt) * val).astype(BF16)
    acc_scr[...] += jnp.dot(act, wd_ref[...], preferred_element_type=F32)

    @pl.when(j == N_F_TILES - 1)
    def _():
        o_ref[...] = x_ref[...] + g2_ref[...] * acc_scr[...]


def _ffn(x1, mod3, norm2_g, w_up, conv_w, conv_b, w_down):
    tm, tf = TM_FFN, TF_FFN
    tiles_per_seq = SEQ // tm
    halo_per_tile = tm // FFN_HALO
    modspec = lambda k: pl.BlockSpec((None, 1, D_MODEL),
                                     lambda i, j: (i // tiles_per_seq * N_MOD + k, 0, 0))
    return pl.pallas_call(
        _ffn_kernel,
        grid=(TOKENS // tm, N_F_TILES),
        in_specs=[
            pl.BlockSpec((tm, D_MODEL), lambda i, j: (i, 0)),
            pl.BlockSpec((FFN_HALO, D_MODEL), lambda i, j: (jnp.maximum(i * halo_per_tile - 1, 0), 0)),
            modspec(4), modspec(3), modspec(5),
            pl.BlockSpec((1, D_MODEL), lambda i, j: (0, 0)),
            pl.BlockSpec((D_MODEL, tf), lambda i, j: (0, j)),
            pl.BlockSpec((D_MODEL, tf), lambda i, j: (0, N_F_TILES + j)),
            pl.BlockSpec((3, tf), lambda i, j: (0, j)),
            pl.BlockSpec((3, tf), lambda i, j: (0, N_F_TILES + j)),
            pl.BlockSpec((1, tf), lambda i, j: (0, j)),
            pl.BlockSpec((1, tf), lambda i, j: (0, N_F_TILES + j)),
            pl.BlockSpec((tf, D_MODEL), lambda i, j: (j, 0)),
        ],
        out_specs=pl.BlockSpec((tm, D_MODEL), lambda i, j: (i, 0)),
        out_shape=jax.ShapeDtypeStruct((TOKENS, D_MODEL), F32),
        scratch_shapes=[pltpu.VMEM((FFN_HALO + tm, D_MODEL), BF16),
                        pltpu.VMEM((tm, D_MODEL), F32)],
        compiler_params=_params("parallel", "arbitrary"),
        name="ffn",
    )(x1, x1, mod3, mod3, mod3, norm2_g, w_up, w_up, conv_w, conv_w, conv_b, conv_b, w_down)


def _block_diag(w):
    _, r, c = w.shape
    g = S5_TILE_GROUPS
    w = w.reshape(S5_TILES, g, r, 1, c)
    eye = jnp.eye(g, dtype=bool).reshape(1, g, 1, g, 1)
    return jnp.where(eye, w, 0.0).reshape(S5_TILES, g * r, g * c)


def _layer(x2, c, pos_col, w_mod, b_mod, norm1_g, w_in, q_norm_g, k_norm_g,
           lam_re, lam_im, log_dt, b_re, b_im, c_re, c_im, d_skip, w_glu, b_glu,
           w_attn_br, w_ssm_br, w_out, norm2_g, w_up, conv_w, conv_b, w_down):
    c_pad = jnp.pad(c, ((0, SUBLANES - BATCH), (0, 0)))
    mod = _mod(c_pad, w_mod, b_mod.reshape(1, -1))[:BATCH]
    mod3 = mod.reshape(BATCH * N_MOD, 1, D_MODEL)

    half = HEAD_DIM // 2
    inv_freq = ROPE_THETA ** (-jnp.arange(half, dtype=F32) / half)
    freq_row = jnp.concatenate([inv_freq, inv_freq]).reshape(1, HEAD_DIM)
    cos, sin = _rope_tables(pos_col, freq_row)

    proj = _in_proj(x2, mod3, norm1_g.reshape(1, -1), w_in.astype(BF16), cos, sin,
                    q_norm_g.reshape(1, -1), k_norm_g.reshape(1, -1))
    y_attn = _attention(proj)

    ns = S5_TILE_STATES
    ldt = jnp.repeat(log_dt, SSM_STATE)
    rows = [a.reshape(S5_TILES, 1, ns) for a in (lam_re, lam_im, ldt)]
    cols = [a.reshape(S5_TILES, ns, 1) for a in (lam_re, lam_im, ldt)]
    btr = _block_diag(b_re.transpose(0, 2, 1))
    bti = _block_diag(b_im.transpose(0, 2, 1))
    ctr = _block_diag(c_re.transpose(0, 2, 1))
    cti = _block_diag(c_im.transpose(0, 2, 1))
    tmat, wst, wout, al = _s5_prep(*rows, *cols, btr, bti, ctr, cti)
    gl = _s5(proj, tmat, wst, wout, al, d_skip.reshape(1, -1))

    x1 = _merge(x2, gl, y_attn, proj, mod3, w_glu.astype(BF16), b_glu.reshape(1, -1),
                w_attn_br.astype(BF16), w_ssm_br.astype(BF16), w_out.astype(BF16))
    return _ffn(x1, mod3, norm2_g.reshape(1, -1), w_up.astype(BF16), conv_w,
                conv_b.reshape(1, -1), w_down.astype(BF16))


def kernel(x, c, positions, w_mod, b_mod, norm1_g, w_in, q_norm_g, k_norm_g, ssm_lambda_re, ssm_lambda_im, ssm_log_dt, ssm_b_re, ssm_b_im, ssm_c_re, ssm_c_im, ssm_d, w_glu, b_glu, w_attn_br, w_ssm_br, w_out, norm2_g, w_up, conv_w, conv_b, w_down):
    assert x.shape == (BATCH, SEQ, D_MODEL) and w_in.shape[0] == 1
    x2 = x.reshape(TOKENS, D_MODEL)
    pos_col = positions.reshape(TOKENS, 1)
    for l in range(w_in.shape[0]):
        x2 = _layer(x2, c, pos_col, w_mod[l], b_mod[l], norm1_g[l], w_in[l], q_norm_g[l], k_norm_g[l],
                    ssm_lambda_re[l], ssm_lambda_im[l], ssm_log_dt[l], ssm_b_re[l], ssm_b_im[l],
                    ssm_c_re[l], ssm_c_im[l], ssm_d[l], w_glu[l], b_glu[l], w_attn_br[l],
                    w_ssm_br[l], w_out[l], norm2_g[l], w_up[l], conv_w[l], conv_b[l], w_down[l])
    return x2.reshape(BATCH, SEQ, D_MODEL)
```

```python
import functools
import math

import jax
import jax.numpy as jnp
from jax import lax
from jax.experimental import pallas as pl
from jax.experimental.pallas import tpu as pltpu

F32 = jnp.float32
BF16 = jnp.bfloat16

D_MODEL = 2048
BATCH = 4
SEQ = 4096
TOKENS = BATCH * SEQ
HEAD_DIM = 128
N_HEADS = 8
ATTN_WIDTH = N_HEADS * HEAD_DIM
MOBA_BLOCK = 256
N_KV_BLOCKS = SEQ // MOBA_BLOCK
MOBA_TOPK = 3
ROPE_THETA = 10000.0
SSM_WIDTH = D_MODEL // 2
SSM_GROUP = 16
SSM_GROUPS = SSM_WIDTH // SSM_GROUP
SSM_STATE = 64
FFN_HIDDEN = 5632
N_MOD = 6
IN_WIDTH = 3 * ATTN_WIDTH + SSM_WIDTH + 2 * D_MODEL
EPS = 1e-6
NEG_INF = -1e30

LANES = 128
SUBLANES = 8
VMEM_LIMIT = 56 * 1024 * 1024

TM_PROJ = 512
TN_PROJ = 1024
ATTN_ONES_ROWS = 16
Q_SCALE = HEAD_DIM ** -0.5 * math.log2(math.e)
TM_MERGE = 256
TM_FFN = 512
TF_FFN = 512
N_F_TILES = FFN_HIDDEN // TF_FFN
FFN_HALO = 16
S5_TILE_GROUPS = LANES // SSM_GROUP
S5_TILES = SSM_GROUPS // S5_TILE_GROUPS
S5_TILE_STATES = S5_TILE_GROUPS * SSM_STATE
S5_CHUNK = 16
S5_CHUNKS = SEQ // S5_CHUNK
S5_FLAT = S5_CHUNK * LANES
S5_TRI_SPLITS = 4

COL_K = ATTN_WIDTH // LANES
COL_V = 2 * ATTN_WIDTH // LANES
COL_U = 3 * ATTN_WIDTH // LANES


def _params(*sem):
    return pltpu.CompilerParams(dimension_semantics=sem, vmem_limit_bytes=VMEM_LIMIT)


def _split_bf16(x):
    hi = x.astype(BF16)
    return hi, (x - hi.astype(F32)).astype(BF16)


def _dot_split(a, b_hl):
    a_hi, a_lo = _split_bf16(a)
    b_hi, b_lo = b_hl
    dot = functools.partial(jnp.dot, preferred_element_type=F32)
    return dot(a_hi, b_hi) + (dot(a_hi, b_lo) + dot(a_lo, b_hi))


def _mod_kernel(c_ref, w_ref, b_ref, o_ref):
    c = c_ref[...]
    sc = c * jax.nn.sigmoid(c)
    o_ref[...] = _dot_split(sc, _split_bf16(w_ref[...])) + b_ref[...]


def _mod(c_pad, w_mod, b_mod):
    tn = 1024
    n = N_MOD * D_MODEL
    return pl.pallas_call(
        _mod_kernel,
        grid=(n // tn,),
        in_specs=[pl.BlockSpec((SUBLANES, D_MODEL), lambda j: (0, 0)),
                  pl.BlockSpec((D_MODEL, tn), lambda j: (0, j)),
                  pl.BlockSpec((1, tn), lambda j: (0, j))],
        out_specs=pl.BlockSpec((SUBLANES, tn), lambda j: (0, j)),
        out_shape=jax.ShapeDtypeStruct((SUBLANES, n), F32),
        compiler_params=_params("parallel"),
        name="mod",
    )(c_pad, w_mod, b_mod)


def _rope_kernel(pos_ref, freq_ref, cos_ref, sin_ref):
    ang = pos_ref[...].astype(F32) * freq_ref[...]
    lane = lax.broadcasted_iota(jnp.int32, ang.shape, 1)
    cos_ref[...] = jnp.cos(ang)
    sin_ref[...] = jnp.where(lane < HEAD_DIM // 2, -1.0, 1.0) * jnp.sin(ang)


def _rope_tables(pos_col, freq_row):
    tr = 2048
    return pl.pallas_call(
        _rope_kernel,
        grid=(TOKENS // tr,),
        in_specs=[pl.BlockSpec((tr, 1), lambda i: (i, 0)),
                  pl.BlockSpec((1, HEAD_DIM), lambda i: (0, 0))],
        out_specs=[pl.BlockSpec((tr, HEAD_DIM), lambda i: (i, 0)),
                   pl.BlockSpec((tr, HEAD_DIM), lambda i: (i, 0))],
        out_shape=[jax.ShapeDtypeStruct((TOKENS, HEAD_DIM), F32)] * 2,
        compiler_params=_params("parallel"),
        name="rope",
    )(pos_col, freq_row)


def _rms_modulate(x, g, scale, shift):
    ms = jnp.mean(x * x, axis=-1, keepdims=True)
    y = x * lax.rsqrt(ms + EPS) * g
    return y * (1.0 + scale) + shift


def _in_proj_kernel(x_ref, sc_ref, sh_ref, g_ref, w_ref, cos_ref, sin_ref, qg_ref, kg_ref,
                    o_ref, h_scr):
    j = pl.program_id(1)

    @pl.when(j == 0)
    def _():
        h_scr[...] = _rms_modulate(x_ref[...], g_ref[...], sc_ref[...], sh_ref[...]).astype(BF16)

    acc = jnp.dot(h_scr[...], w_ref[...], preferred_element_type=F32)

    @pl.when(j < 2)
    def _():
        g = jnp.where(j == 0, qg_ref[...], kg_ref[...])
        out_scale = jnp.where(j == 0, Q_SCALE, 1.0)
        cos = cos_ref[...] * out_scale
        sin = sin_ref[...] * out_scale
        for hd in range(N_HEADS):
            blk = acc[:, hd * HEAD_DIM:(hd + 1) * HEAD_DIM]
            ms = jnp.mean(blk * blk, axis=-1, keepdims=True)
            y = blk * lax.rsqrt(ms + EPS) * g
            rot = y * cos + pltpu.roll(y, HEAD_DIM // 2, axis=1) * sin
            o_ref[:, hd * HEAD_DIM:(hd + 1) * HEAD_DIM] = rot.astype(BF16)

    @pl.when(j >= 2)
    def _():
        o_ref[...] = acc.astype(BF16)


def _in_proj(x2, mod3, norm1_g, w_in_bf, cos, sin, qg, kg):
    tiles_per_seq = SEQ // TM_PROJ
    return pl.pallas_call(
        _in_proj_kernel,
        grid=(TOKENS // TM_PROJ, IN_WIDTH // TN_PROJ),
        in_specs=[
            pl.BlockSpec((TM_PROJ, D_MODEL), lambda i, j: (i, 0)),
            pl.BlockSpec((None, 1, D_MODEL), lambda i, j: (i // tiles_per_seq * N_MOD + 1, 0, 0)),
            pl.BlockSpec((None, 1, D_MODEL), lambda i, j: (i // tiles_per_seq * N_MOD + 0, 0, 0)),
            pl.BlockSpec((1, D_MODEL), lambda i, j: (0, 0)),
            pl.BlockSpec((D_MODEL, TN_PROJ), lambda i, j: (0, j)),
            pl.BlockSpec((TM_PROJ, HEAD_DIM), lambda i, j: (i, 0)),
            pl.BlockSpec((TM_PROJ, HEAD_DIM), lambda i, j: (i, 0)),
            pl.BlockSpec((1, HEAD_DIM), lambda i, j: (0, 0)),
            pl.BlockSpec((1, HEAD_DIM), lambda i, j: (0, 0)),
        ],
        out_specs=pl.BlockSpec((TM_PROJ, TN_PROJ), lambda i, j: (i, j)),
        out_shape=jax.ShapeDtypeStruct((TOKENS, IN_WIDTH), BF16),
        scratch_shapes=[pltpu.VMEM((TM_PROJ, D_MODEL), BF16)],
        compiler_params=_params("parallel", "arbitrary"),
        name="in_proj",
    )(x2, mod3, mod3, norm1_g, w_in_bf, cos, sin, qg, kg)


_NT = (((1,), (1,)), ((), ()))


def _attn_kernel(q_ref, k_ref, v_ref, o_ref, kmean_scr, vt_scr, s_scr):
    i = pl.program_id(2)

    @pl.when(i == 0)
    def _():
        for n in range(N_KV_BLOCKS):
            rows = slice(n * MOBA_BLOCK, (n + 1) * MOBA_BLOCK)
            kmean_scr[n:n + 1, :] = jnp.mean(k_ref[rows, :].astype(F32), axis=0, keepdims=True)
            vt_scr[n, 0:HEAD_DIM, :] = v_ref[rows, :].astype(F32).T.astype(BF16)
            vt_scr[n, HEAD_DIM:, :] = jnp.ones((ATTN_ONES_ROWS, MOBA_BLOCK), BF16)

    def fold8(t):
        return t.reshape(MOBA_BLOCK // SUBLANES, SUBLANES, MOBA_BLOCK)

    def rows_of(n):
        return slice(n * MOBA_BLOCK, (n + 1) * MOBA_BLOCK)

    class QueryBlock:
        def __init__(self, which, blk_idx, km_parts):
            self.idx = blk_idx
            self.base = which * (N_KV_BLOCKS + 1)
            self.q = q_ref[rows_of(which), :]
            self.out_rows = rows_of(which)
            lhs = jnp.concatenate([k_ref[rows_of(blk_idx), :], *km_parts], axis=0)
            res = lax.dot_general(lhs, self.q, _NT, preferred_element_type=F32)
            nb = N_KV_BLOCKS
            bs = (res[MOBA_BLOCK:MOBA_BLOCK + nb] + res[MOBA_BLOCK + nb:MOBA_BLOCK + 2 * nb]
                  + res[MOBA_BLOCK + 2 * nb:MOBA_BLOCK + 3 * nb])
            blk = lax.broadcasted_iota(jnp.int32, bs.shape, 0)
            past = blk < blk_idx
            bs = jnp.where(past, bs, NEG_INF)
            rank = jnp.zeros(bs.shape, F32)
            for m in range(N_KV_BLOCKS):
                row = bs[m:m + 1, :]
                tie = jnp.where(blk > m, 1.0, 0.0)
                rank = rank + jnp.where(row > bs, 1.0, jnp.where(row == bs, tie, 0.0))
            self.bias = jnp.where(past, jnp.where(rank < MOBA_TOPK, 0.0, NEG_INF), NEG_INF)
            s = res[0:MOBA_BLOCK]
            ki = lax.broadcasted_iota(jnp.int32, s.shape, 0)
            qi = lax.broadcasted_iota(jnp.int32, s.shape, 1)
            s = jnp.where(ki <= qi, s, NEG_INF)
            s_scr[rows_of(self.base + N_KV_BLOCKS), :] = s
            self.m8 = jnp.max(fold8(s), axis=0)
            self.acc = [None, None]

        def pass1_tile(self, n):
            sn = lax.dot_general(k_ref[rows_of(n), :], self.q, _NT, preferred_element_type=F32)
            s_scr[rows_of(self.base + n), :] = sn
            self.m8 = jnp.maximum(self.m8, jnp.max(fold8(sn), axis=0) + self.bias[n:n + 1, :])

        def finish_pass1(self):
            self.m = jnp.max(self.m8, axis=0, keepdims=True)
            self._accumulate(self.idx, N_KV_BLOCKS, self.m)

        def _accumulate(self, v_blk, slab, shift):
            p = jnp.exp2(s_scr[rows_of(self.base + slab), :] - shift)
            pv = jnp.dot(vt_scr[v_blk], p.astype(BF16), preferred_element_type=F32)
            k = v_blk % 2
            self.acc[k] = pv if self.acc[k] is None else self.acc[k] + pv

        def pass2_tile(self, n):
            self._accumulate(n, n, self.m - self.bias[n:n + 1, :])

        def store(self):
            acc = self.acc[0] if self.acc[1] is None else (
                self.acc[1] if self.acc[0] is None else self.acc[0] + self.acc[1])
            l = acc[HEAD_DIM:HEAD_DIM + 1, :]
            o_ref[self.out_rows, :] = (acc[0:HEAD_DIM, :] / l).T.astype(BF16)

    def attend_pair(pair):
        km = kmean_scr[...]
        km_hi = km.astype(BF16)
        rest = km - km_hi.astype(F32)
        km_mid = rest.astype(BF16)
        km_lo = (rest - km_mid.astype(F32)).astype(BF16)
        parts = (km_hi, km_mid, km_lo)
        a = QueryBlock(0, 2 * pair, parts)
        for n in range(a.idx):
            a.pass1_tile(n)
        a.finish_pass1()
        b = QueryBlock(1, 2 * pair + 1, parts)
        for n in range(b.idx):
            b.pass1_tile(n)
            if n < a.idx:
                a.pass2_tile(n)
        a.store()
        b.finish_pass1()
        for n in range(b.idx):
            b.pass2_tile(n)
        b.store()

    for pair in range(N_KV_BLOCKS // 2):
        pl.when(i == pair)(functools.partial(attend_pair, pair))


def _attention(proj):
    pairs = N_KV_BLOCKS // 2
    return pl.pallas_call(
        _attn_kernel,
        grid=(BATCH, N_HEADS, pairs),
        in_specs=[
            pl.BlockSpec((2 * MOBA_BLOCK, HEAD_DIM), lambda b, h, i: (b * pairs + i, h)),
            pl.BlockSpec((SEQ, HEAD_DIM), lambda b, h, i: (b, COL_K + h)),
            pl.BlockSpec((SEQ, HEAD_DIM), lambda b, h, i: (b, COL_V + h)),
        ],
        out_specs=pl.BlockSpec((2 * MOBA_BLOCK, HEAD_DIM), lambda b, h, i: (b * pairs + i, h)),
        out_shape=jax.ShapeDtypeStruct((TOKENS, ATTN_WIDTH), BF16),
        scratch_shapes=[pltpu.VMEM((N_KV_BLOCKS, HEAD_DIM), F32),
                        pltpu.VMEM((N_KV_BLOCKS, HEAD_DIM + ATTN_ONES_ROWS, MOBA_BLOCK), BF16),
                        pltpu.VMEM((2 * (N_KV_BLOCKS + 1) * MOBA_BLOCK, MOBA_BLOCK), F32)],
        compiler_params=_params("parallel", "parallel", "arbitrary"),
        name="moba_attn",
    )(proj, proj, proj)


def _s5_prep_kernel(lr_r, li_r, ldt_r, lr_c, li_c, ldt_c, btr_ref, bti_ref, ctr_ref, cti_ref,
                    tm_ref, wst_ref, wout_ref, al_ref):
    L = S5_CHUNK
    ns = S5_TILE_STATES
    lr = lr_r[...]
    li = li_r[...]
    dt = jnp.exp(ldt_r[...])
    mag = jnp.exp(lr * dt)
    ab_re = mag * jnp.cos(li * dt)
    ab_im = mag * jnp.sin(li * dt)
    den = lr * lr + li * li
    nr = ab_re - 1.0
    ni = ab_im
    coef_re = (nr * lr + ni * li) / den
    coef_im = (ni * lr - nr * li) / den

    tau = jnp.minimum(lax.broadcasted_iota(jnp.int32, (4 * SUBLANES, ns), 0), L).astype(F32)
    pmag = jnp.exp(lr * dt * tau)
    pang = li * dt * tau
    p_re = pmag * jnp.cos(pang)
    p_im = pmag * jnp.sin(pang)
    w_re = coef_re * p_re - coef_im * p_im
    w_im = coef_re * p_im + coef_im * p_re

    btr = btr_ref[...]
    bti = bti_ref[...]
    ctr = ctr_ref[...]
    cti = cti_ref[...]

    ctr_hl = _split_bf16(ctr)
    cti_hl = _split_bf16(cti)
    tm_ref[...] = jnp.zeros(tm_ref.shape, tm_ref.dtype)
    for t in range(L):
        wr = w_re[t:t + 1, :]
        wi = w_im[t:t + 1, :]
        s_re = btr * wr - bti * wi
        s_im = btr * wi + bti * wr
        rp = L - 1 - t
        wst_ref[rp * LANES:(rp + 1) * LANES, 0:ns] = s_re.astype(BF16)
        wst_ref[rp * LANES:(rp + 1) * LANES, ns:2 * ns] = s_im.astype(BF16)
        kt = _dot_split(s_re, ctr_hl) - _dot_split(s_im, cti_hl)
        ktb = kt.astype(BF16)
        for r0 in range(L - t):
            tm_ref[r0 * LANES:(r0 + 1) * LANES, (r0 + t) * LANES:(r0 + t + 1) * LANES] = ktb

    al_ref[:, 0:ns] = p_re[L:L + 1, :]
    al_ref[:, ns:2 * ns] = p_im[L:L + 1, :]

    lrc = lr_c[...]
    lic = li_c[...]
    dtc = jnp.exp(ldt_c[...])
    tauc = (jnp.minimum(lax.broadcasted_iota(jnp.int32, (ns, LANES), 1), L - 1) + 1).astype(F32)
    cmag = jnp.exp(lrc * dtc * tauc)
    cang = lic * dtc * tauc
    a_re = cmag * jnp.cos(cang)
    a_im = cmag * jnp.sin(cang)
    for r in range(L):
        ar = a_re[:, r:r + 1]
        ai = a_im[:, r:r + 1]
        wout_ref[0:ns, r * LANES:(r + 1) * LANES] = (ctr * ar - cti * ai).astype(BF16)
        wout_ref[ns:2 * ns, r * LANES:(r + 1) * LANES] = (-(ctr * ai + cti * ar)).astype(BF16)


def _s5_prep(lr_r, li_r, ldt_r, lr_c, li_c, ldt_c, btr, bti, ctr, cti):
    ns = S5_TILE_STATES
    row = pl.BlockSpec((None, 1, ns), lambda t: (t, 0, 0))
    col = pl.BlockSpec((None, ns, 1), lambda t: (t, 0, 0))
    bt = pl.BlockSpec((None, LANES, ns), lambda t: (t, 0, 0))
    ct = pl.BlockSpec((None, ns, LANES), lambda t: (t, 0, 0))
    return pl.pallas_call(
        _s5_prep_kernel,
        grid=(S5_TILES,),
        in_specs=[row, row, row, col, col, col, bt, bt, ct, ct],
        out_specs=[pl.BlockSpec((None, S5_FLAT, S5_FLAT), lambda t: (t, 0, 0)),
                   pl.BlockSpec((None, S5_FLAT, 2 * ns), lambda t: (t, 0, 0)),
                   pl.BlockSpec((None, 2 * ns, S5_FLAT), lambda t: (t, 0, 0)),
                   pl.BlockSpec((None, 1, 2 * ns), lambda t: (t, 0, 0))],
        out_shape=[jax.ShapeDtypeStruct((S5_TILES, S5_FLAT, S5_FLAT), BF16),
                   jax.ShapeDtypeStruct((S5_TILES, S5_FLAT, 2 * ns), BF16),
                   jax.ShapeDtypeStruct((S5_TILES, 2 * ns, S5_FLAT), BF16),
                   jax.ShapeDtypeStruct((S5_TILES, 1, 2 * ns), F32)],
        compiler_params=_params("parallel"),
        name="s5_prep",
    )(lr_r, li_r, ldt_r, lr_c, li_c, ldt_c, btr, bti, ctr, cti)


def _s5_kernel(u_ref, tm_ref, wst_ref, wout_ref, al_ref, d_ref, o_ref,
               uf_scr, uflat_scr, xs_scr, xp_scr, y_scr):
    L = S5_CHUNK
    ns = S5_TILE_STATES
    uf_scr[...] = u_ref[...].astype(F32)
    for r in range(L):
        uflat_scr[:, r * LANES:(r + 1) * LANES] = uf_scr[pl.ds(r, S5_CHUNKS, stride=L), :].astype(BF16)
    uflat = uflat_scr[...]

    xs_scr[...] = jnp.dot(uflat, wst_ref[...], preferred_element_type=F32)
    al = al_ref[...]
    al_re = al[:, 0:ns]
    al_im = al[:, ns:2 * ns]

    def body(c, state):
        s_re, s_im = state
        xp_scr[pl.ds(c, 1), 0:ns] = s_re
        xp_scr[pl.ds(c, 1), ns:2 * ns] = s_im
        x_re = xs_scr[pl.ds(c, 1), 0:ns]
        x_im = xs_scr[pl.ds(c, 1), ns:2 * ns]
        return (al_re * s_re - al_im * s_im + x_re, al_re * s_im + al_im * s_re + x_im)

    zero = jnp.zeros((1, ns), F32)
    lax.fori_loop(0, S5_CHUNKS, body, (zero, zero))

    xp = xp_scr[...].astype(BF16)
    step = S5_FLAT // S5_TRI_SPLITS
    for c0 in range(0, S5_FLAT, step):
        c1 = c0 + step
        y_scr[:, c0:c1] = (
            jnp.dot(uflat_scr[:, 0:c1], tm_ref[0:c1, c0:c1], preferred_element_type=F32)
            + jnp.dot(xp, wout_ref[:, c0:c1], preferred_element_type=F32))
    d = d_ref[...]
    for r in range(L):
        ur = uf_scr[pl.ds(r, S5_CHUNKS, stride=L), :]
        yr = y_scr[:, r * LANES:(r + 1) * LANES] + d * ur
        uf_scr[pl.ds(r, S5_CHUNKS, stride=L), :] = jax.nn.gelu(yr, approximate=True)
    o_ref[...] = uf_scr[...].astype(BF16)


def _s5(proj, tmat, wst, wout, al, d_row):
    ns = S5_TILE_STATES
    return pl.pallas_call(
        _s5_kernel,
        grid=(S5_TILES, BATCH),
        in_specs=[
            pl.BlockSpec((SEQ, LANES), lambda t, b: (b, COL_U + t)),
            pl.BlockSpec((None, S5_FLAT, S5_FLAT), lambda t, b: (t, 0, 0)),
            pl.BlockSpec((None, S5_FLAT, 2 * ns), lambda t, b: (t, 0, 0)),
            pl.BlockSpec((None, 2 * ns, S5_FLAT), lambda t, b: (t, 0, 0)),
            pl.BlockSpec((None, 1, 2 * ns), lambda t, b: (t, 0, 0)),
            pl.BlockSpec((1, LANES), lambda t, b: (0, t)),
        ],
        out_specs=pl.BlockSpec((SEQ, LANES), lambda t, b: (b, t)),
        out_shape=jax.ShapeDtypeStruct((TOKENS, SSM_WIDTH), BF16),
        scratch_shapes=[pltpu.VMEM((SEQ, LANES), F32),
                        pltpu.VMEM((S5_CHUNKS, S5_FLAT), BF16),
                        pltpu.VMEM((S5_CHUNKS, 2 * ns), F32),
                        pltpu.VMEM((S5_CHUNKS, 2 * ns), F32),
                        pltpu.VMEM((S5_CHUNKS, S5_FLAT), F32)],
        compiler_params=_params("parallel", "parallel"),
        name="s5_scan",
    )(proj, tmat, wst, wout, al, d_row)


def _merge_kernel(x_ref, gl_ref, ya_ref, ga_ref, gs_ref, g1_ref, wglu_ref, bglu_ref,
                  wa_ref, ws_ref, wo_ref, o_ref):
    gl = gl_ref[...]
    z = jnp.dot(gl, wglu_ref[...], preferred_element_type=F32) + bglu_ref[...]
    y_ssm = (gl.astype(F32) * jax.nn.sigmoid(z)).astype(BF16)
    a = jnp.dot(ya_ref[...], wa_ref[...], preferred_element_type=F32)
    s = jnp.dot(y_ssm, ws_ref[...], preferred_element_type=F32)
    merged = (jax.nn.sigmoid(ga_ref[...].astype(F32)) * a
              + jax.nn.sigmoid(gs_ref[...].astype(F32)) * s)
    out = jnp.dot(merged.astype(BF16), wo_ref[...], preferred_element_type=F32)
    o_ref[...] = x_ref[...] + g1_ref[...] * out


def _merge(x2, gl, y_attn, proj, mod3, w_glu, b_glu, w_attn_br, w_ssm_br, w_out):
    tm = TM_MERGE
    tiles_per_seq = SEQ // tm
    gate_blk = (3 * ATTN_WIDTH + SSM_WIDTH) // D_MODEL
    const = lambda shape: pl.BlockSpec(shape, lambda i: (0, 0), pipeline_mode=pl.Buffered(1))
    return pl.pallas_call(
        _merge_kernel,
        grid=(TOKENS // tm,),
        in_specs=[
            pl.BlockSpec((tm, D_MODEL), lambda i: (i, 0)),
            pl.BlockSpec((tm, SSM_WIDTH), lambda i: (i, 0)),
            pl.BlockSpec((tm, ATTN_WIDTH), lambda i: (i, 0)),
            pl.BlockSpec((tm, D_MODEL), lambda i: (i, gate_blk)),
            pl.BlockSpec((tm, D_MODEL), lambda i: (i, gate_blk + 1)),
            pl.BlockSpec((None, 1, D_MODEL), lambda i: (i // tiles_per_seq * N_MOD + 2, 0, 0)),
            const((SSM_WIDTH, SSM_WIDTH)),
            const((1, SSM_WIDTH)),
            const((ATTN_WIDTH, D_MODEL)),
            const((SSM_WIDTH, D_MODEL)),
            const((D_MODEL, D_MODEL)),
        ],
        out_specs=pl.BlockSpec((tm, D_MODEL), lambda i: (i, 0)),
        out_shape=jax.ShapeDtypeStruct((TOKENS, D_MODEL), F32),
        compiler_params=_params("parallel"),
        name="merge",
    )(x2, gl, y_attn, proj, proj, mod3, w_glu, b_glu, w_attn_br, w_ssm_br, w_out)


def _ffn_kernel(x_ref, halo_ref, sc_ref, sh_ref, g2_ref, ng_ref, wv_ref, wg_ref,
                cwv_ref, cwg_ref, cbv_ref, cbg_ref, wd_ref, o_ref, h_scr, acc_scr):
    i = pl.program_id(0)
    j = pl.program_id(1)
    tm = TM_FFN

    @pl.when(j == 0)
    def _():
        g = ng_ref[...]
        sc = sc_ref[...]
        sh = sh_ref[...]
        h_scr[FFN_HALO:, :] = _rms_modulate(x_ref[...], g, sc, sh).astype(BF16)
        halo = _rms_modulate(halo_ref[...], g, sc, sh)
        seq_start = (i % (SEQ // tm)) == 0
        h_scr[0:FFN_HALO, :] = jnp.where(seq_start, 0.0, halo).astype(BF16)
        acc_scr[...] = jnp.zeros(acc_scr.shape, acc_scr.dtype)

    h = h_scr[...]

    def conv(w_ref, cw_ref, cb_ref):
        up = jnp.dot(h, w_ref[...], preferred_element_type=F32)
        cw = cw_ref[...]
        up1 = pltpu.roll(up, 1, axis=0)
        up2 = pltpu.roll(up, 2, axis=0)
        y = cw[2:3, :] * up + cw[1:2, :] * up1 + cw[0:1, :] * up2 + cb_ref[...]
        return y[FFN_HALO:, :]

    val = conv(wv_ref, cwv_ref, cbv_ref)
    gt = conv(wg_ref, cwg_ref, cbg_ref)
    act = (gt * jax.nn.sigmoid(gt) * val).astype(BF16)
    acc_scr[...] += jnp.dot(act, wd_ref[...], preferred_element_type=F32)

    @pl.when(j == N_F_TILES - 1)
    def _():
        o_ref[...] = x_ref[...] + g2_ref[...] * acc_scr[...]


def _ffn(x1, mod3, norm2_g, w_up, conv_w, conv_b, w_down):
    tm, tf = TM_FFN, TF_FFN
    tiles_per_seq = SEQ // tm
    halo_per_tile = tm // FFN_HALO
    modspec = lambda k: pl.BlockSpec((None, 1, D_MODEL),
                                     lambda i, j: (i // tiles_per_seq * N_MOD + k, 0, 0))
    return pl.pallas_call(
        _ffn_kernel,
        grid=(TOKENS // tm, N_F_TILES),
        in_specs=[
            pl.BlockSpec((tm, D_MODEL), lambda i, j: (i, 0)),
            pl.BlockSpec((FFN_HALO, D_MODEL), lambda i, j: (jnp.maximum(i * halo_per_tile - 1, 0), 0)),
            modspec(4), modspec(3), modspec(5),
            pl.BlockSpec((1, D_MODEL), lambda i, j: (0, 0)),
            pl.BlockSpec((D_MODEL, tf), lambda i, j: (0, j)),
            pl.BlockSpec((D_MODEL, tf), lambda i, j: (0, N_F_TILES + j)),
            pl.BlockSpec((3, tf), lambda i, j: (0, j)),
            pl.BlockSpec((3, tf), lambda i, j: (0, N_F_TILES + j)),
            pl.BlockSpec((1, tf), lambda i, j: (0, j)),
            pl.BlockSpec((1, tf), lambda i, j: (0, N_F_TILES + j)),
            pl.BlockSpec((tf, D_MODEL), lambda i, j: (j, 0)),
        ],
        out_specs=pl.BlockSpec((tm, D_MODEL), lambda i, j: (i, 0)),
        out_shape=jax.ShapeDtypeStruct((TOKENS, D_MODEL), F32),
        scratch_shapes=[pltpu.VMEM((FFN_HALO + tm, D_MODEL), BF16),
                        pltpu.VMEM((tm, D_MODEL), F32)],
        compiler_params=_params("parallel", "arbitrary"),
        name="ffn",
    )(x1, x1, mod3, mod3, mod3, norm2_g, w_up, w_up, conv_w, conv_w, conv_b, conv_b, w_down)


def _block_diag(w):
    _, r, c = w.shape
    g = S5_TILE_GROUPS
    w = w.reshape(S5_TILES, g, r, 1, c)
    eye = jnp.eye(g, dtype=bool).reshape(1, g, 1, g, 1)
    return jnp.where(eye, w, 0.0).reshape(S5_TILES, g * r, g * c)


def _s5_branch(proj, lam_re, lam_im, log_dt, b_re, b_im, c_re, c_im, d_skip):
    ns = S5_TILE_STATES
    ldt = jnp.repeat(log_dt, SSM_STATE)
    rows = [a.reshape(S5_TILES, 1, ns) for a in (lam_re, lam_im, ldt)]
    cols = [a.reshape(S5_TILES, ns, 1) for a in (lam_re, lam_im, ldt)]
    btr = _block_diag(b_re.transpose(0, 2, 1))
    bti = _block_diag(b_im.transpose(0, 2, 1))
    ctr = _block_diag(c_re.transpose(0, 2, 1))
    cti = _block_diag(c_im.transpose(0, 2, 1))
    tmat, wst, wout, al = _s5_prep(*rows, *cols, btr, bti, ctr, cti)
    return _s5(proj, tmat, wst, wout, al, d_skip.reshape(1, -1))


def _layer(x2, c, pos_col, w_mod, b_mod, norm1_g, w_in, q_norm_g, k_norm_g,
           lam_re, lam_im, log_dt, b_re, b_im, c_re, c_im, d_skip, w_glu, b_glu,
           w_attn_br, w_ssm_br, w_out, norm2_g, w_up, conv_w, conv_b, w_down):
    c_pad = jnp.pad(c, ((0, SUBLANES - BATCH), (0, 0)))
    mod = _mod(c_pad, w_mod, b_mod.reshape(1, -1))[:BATCH]
    mod3 = mod.reshape(BATCH * N_MOD, 1, D_MODEL)

    half = HEAD_DIM // 2
    inv_freq = ROPE_THETA ** (-jnp.arange(half, dtype=F32) / half)
    freq_row = jnp.concatenate([inv_freq, inv_freq]).reshape(1, HEAD_DIM)
    cos, sin = _rope_tables(pos_col, freq_row)

    proj = _in_proj(x2, mod3, norm1_g.reshape(1, -1), w_in.astype(BF16), cos, sin,
                    q_norm_g.reshape(1, -1), k_norm_g.reshape(1, -1))
    y_attn = _attention(proj)

    gl = _s5_branch(proj, lam_re, lam_im, log_dt, b_re, b_im, c_re, c_im, d_skip)

    x1 = _merge(x2, gl, y_attn, proj, mod3, w_glu.astype(BF16), b_glu.reshape(1, -1),
                w_attn_br.astype(BF16), w_ssm_br.astype(BF16), w_out.astype(BF16))
    return _ffn(x1, mod3, norm2_g.reshape(1, -1), w_up.astype(BF16), conv_w,
                conv_b.reshape(1, -1), w_down.astype(BF16))


def kernel(x, c, positions, w_mod, b_mod, norm1_g, w_in, q_norm_g, k_norm_g, ssm_lambda_re, ssm_lambda_im, ssm_log_dt, ssm_b_re, ssm_b_im, ssm_c_re, ssm_c_im, ssm_d, w_glu, b_glu, w_attn_br, w_ssm_br, w_out, norm2_g, w_up, conv_w, conv_b, w_down):
    assert x.shape == (BATCH, SEQ, D_MODEL) and w_in.shape[0] == 1
    x2 = x.reshape(TOKENS, D_MODEL)
    pos_col = positions.reshape(TOKENS, 1)
    for l in range(w_in.shape[0]):
        x2 = _layer(x2, c, pos_col, w_mod[l], b_mod[l], norm1_g[l], w_in[l], q_norm_g[l], k_norm_g[l],
                    ssm_lambda_re[l], ssm_lambda_im[l], ssm_log_dt[l], ssm_b_re[l], ssm_b_im[l],
                    ssm_c_re[l], ssm_c_im[l], ssm_d[l], w_glu[l], b_glu[l], w_attn_br[l],
                    w_ssm_br[l], w_out[l], norm2_g[l], w_up[l], conv_w[l], conv_b[l], w_down[l])
    return x2.reshape(BATCH, SEQ, D_MODEL)
```

```python
import functools
import math

import jax
import jax.numpy as jnp
from jax import lax
from jax.experimental import pallas as pl
from jax.experimental.pallas import tpu as pltpu

F32 = jnp.float32
BF16 = jnp.bfloat16

D_MODEL = 2048
BATCH = 4
SEQ = 4096
TOKENS = BATCH * SEQ
HEAD_DIM = 128
N_HEADS = 8
ATTN_WIDTH = N_HEADS * HEAD_DIM
MOBA_BLOCK = 256
N_KV_BLOCKS = SEQ // MOBA_BLOCK
MOBA_TOPK = 3
ROPE_THETA = 10000.0
SSM_WIDTH = D_MODEL // 2
SSM_GROUP = 16
SSM_GROUPS = SSM_WIDTH // SSM_GROUP
SSM_STATE = 64
FFN_HIDDEN = 5632
N_MOD = 6
IN_WIDTH = 3 * ATTN_WIDTH + SSM_WIDTH + 2 * D_MODEL
EPS = 1e-6
NEG_INF = -1e30

LANES = 128
SUBLANES = 8
VMEM_LIMIT = 56 * 1024 * 1024

TM_PROJ = 512
TN_PROJ = 1024
ATTN_ONES_ROWS = 16
Q_SCALE = HEAD_DIM ** -0.5 * math.log2(math.e)
TM_MERGE = 256
TM_FFN = 512
TF_FFN = 512
N_F_TILES = FFN_HIDDEN // TF_FFN
FFN_HALO = 16
S5_TILE_GROUPS = LANES // SSM_GROUP
S5_TILES = SSM_GROUPS // S5_TILE_GROUPS
S5_TILE_STATES = S5_TILE_GROUPS * SSM_STATE
S5_CHUNK = 16
S5_CHUNKS = SEQ // S5_CHUNK
S5_FLAT = S5_CHUNK * LANES
S5_TRI_SPLITS = 4

COL_K = ATTN_WIDTH // LANES
COL_V = 2 * ATTN_WIDTH // LANES
COL_U = 3 * ATTN_WIDTH // LANES


def _params(*sem):
    return pltpu.CompilerParams(dimension_semantics=sem, vmem_limit_bytes=VMEM_LIMIT)


def _split_bf16(x):
    hi = x.astype(BF16)
    return hi, (x - hi.astype(F32)).astype(BF16)


def _dot_split(a, b_hl):
    a_hi, a_lo = _split_bf16(a)
    b_hi, b_lo = b_hl
    dot = functools.partial(jnp.dot, preferred_element_type=F32)
    return dot(a_hi, b_hi) + (dot(a_hi, b_lo) + dot(a_lo, b_hi))


def _mod_kernel(c_ref, w_ref, b_ref, o_ref):
    c = c_ref[...]
    sc = c * jax.nn.sigmoid(c)
    o_ref[...] = _dot_split(sc, _split_bf16(w_ref[...])) + b_ref[...]


def _mod(c_pad, w_mod, b_mod):
    tn = 1024
    n = N_MOD * D_MODEL
    return pl.pallas_call(
        _mod_kernel,
        grid=(n // tn,),
        in_specs=[pl.BlockSpec((SUBLANES, D_MODEL), lambda j: (0, 0)),
                  pl.BlockSpec((D_MODEL, tn), lambda j: (0, j)),
                  pl.BlockSpec((1, tn), lambda j: (0, j))],
        out_specs=pl.BlockSpec((SUBLANES, tn), lambda j: (0, j)),
        out_shape=jax.ShapeDtypeStruct((SUBLANES, n), F32),
        compiler_params=_params("parallel"),
        name="mod",
    )(c_pad, w_mod, b_mod)


def _rope_kernel(pos_ref, freq_ref, cos_ref, sin_ref):
    ang = pos_ref[...].astype(F32) * freq_ref[...]
    lane = lax.broadcasted_iota(jnp.int32, ang.shape, 1)
    cos_ref[...] = jnp.cos(ang)
    sin_ref[...] = jnp.where(lane < HEAD_DIM // 2, -1.0, 1.0) * jnp.sin(ang)


def _rope_tables(pos_col, freq_row):
    tr = 2048
    return pl.pallas_call(
        _rope_kernel,
        grid=(TOKENS // tr,),
        in_specs=[pl.BlockSpec((tr, 1), lambda i: (i, 0)),
                  pl.BlockSpec((1, HEAD_DIM), lambda i: (0, 0))],
        out_specs=[pl.BlockSpec((tr, HEAD_DIM), lambda i: (i, 0)),
                   pl.BlockSpec((tr, HEAD_DIM), lambda i: (i, 0))],
        out_shape=[jax.ShapeDtypeStruct((TOKENS, HEAD_DIM), F32)] * 2,
        compiler_params=_params("parallel"),
        name="rope",
    )(pos_col, freq_row)


def _rms_modulate(x, g, scale, shift):
    ms = jnp.mean(x * x, axis=-1, keepdims=True)
    y = x * lax.rsqrt(ms + EPS) * g
    return y * (1.0 + scale) + shift


def _in_proj_kernel(x_ref, sc_ref, sh_ref, g_ref, w_ref, cos_ref, sin_ref, qg_ref, kg_ref,
                    o_ref, h_scr):
    j = pl.program_id(1)

    @pl.when(j == 0)
    def _():
        h_scr[...] = _rms_modulate(x_ref[...], g_ref[...], sc_ref[...], sh_ref[...]).astype(BF16)

    acc = jnp.dot(h_scr[...], w_ref[...], preferred_element_type=F32)

    @pl.when(j < 2)
    def _():
        half = HEAD_DIM // 2
        g = jnp.where(j == 0, qg_ref[...], kg_ref[...])
        out_scale = jnp.where(j == 0, Q_SCALE, 1.0)
        c_tab = cos_ref[...] * (g * out_scale)
        s_tab = sin_ref[...] * (pltpu.roll(g, half, axis=1) * out_scale)
        for hd in range(N_HEADS):
            cols = slice(hd * HEAD_DIM, (hd + 1) * HEAD_DIM)
            t = acc[:, cols]
            rs = lax.rsqrt(jnp.mean(t * t, axis=-1, keepdims=True) + EPS)
            o_ref[:, cols] = (rs * (t * c_tab + pltpu.roll(t, half, axis=1) * s_tab)).astype(BF16)

    @pl.when(j >= 2)
    def _():
        o_ref[...] = acc.astype(BF16)


def _in_proj(x2, mod3, norm1_g, w_in_bf, cos, sin, qg, kg):
    tiles_per_seq = SEQ // TM_PROJ
    return pl.pallas_call(
        _in_proj_kernel,
        grid=(TOKENS // TM_PROJ, IN_WIDTH // TN_PROJ),
        in_specs=[
            pl.BlockSpec((TM_PROJ, D_MODEL), lambda i, j: (i, 0)),
            pl.BlockSpec((None, 1, D_MODEL), lambda i, j: (i // tiles_per_seq * N_MOD + 1, 0, 0)),
            pl.BlockSpec((None, 1, D_MODEL), lambda i, j: (i // tiles_per_seq * N_MOD + 0, 0, 0)),
            pl.BlockSpec((1, D_MODEL), lambda i, j: (0, 0)),
            pl.BlockSpec((D_MODEL, TN_PROJ), lambda i, j: (0, j)),
            pl.BlockSpec((TM_PROJ, HEAD_DIM), lambda i, j: (i, 0)),
            pl.BlockSpec((TM_PROJ, HEAD_DIM), lambda i, j: (i, 0)),
            pl.BlockSpec((1, HEAD_DIM), lambda i, j: (0, 0)),
            pl.BlockSpec((1, HEAD_DIM), lambda i, j: (0, 0)),
        ],
        out_specs=pl.BlockSpec((TM_PROJ, TN_PROJ), lambda i, j: (i, j)),
        out_shape=jax.ShapeDtypeStruct((TOKENS, IN_WIDTH), BF16),
        scratch_shapes=[pltpu.VMEM((TM_PROJ, D_MODEL), BF16)],
        compiler_params=_params("parallel", "arbitrary"),
        name="in_proj",
    )(x2, mod3, mod3, norm1_g, w_in_bf, cos, sin, qg, kg)


_NT = (((1,), (1,)), ((), ()))


def _attn_kernel(q_ref, k_ref, v_ref, o_ref, kmean_scr, vt_scr, s_scr):
    i = pl.program_id(2)

    @pl.when(i == 0)
    def _():
        for n in range(N_KV_BLOCKS):
            rows = slice(n * MOBA_BLOCK, (n + 1) * MOBA_BLOCK)
            kmean_scr[n:n + 1, :] = jnp.mean(k_ref[rows, :].astype(F32), axis=0, keepdims=True)
            vt_scr[n, 0:HEAD_DIM, :] = v_ref[rows, :].astype(F32).T.astype(BF16)
            vt_scr[n, HEAD_DIM:, :] = jnp.ones((ATTN_ONES_ROWS, MOBA_BLOCK), BF16)

    def fold8(t):
        return t.reshape(MOBA_BLOCK // SUBLANES, SUBLANES, MOBA_BLOCK)

    def rows_of(n):
        return slice(n * MOBA_BLOCK, (n + 1) * MOBA_BLOCK)

    class QueryBlock:
        def __init__(self, which, blk_idx, km_parts):
            self.idx = blk_idx
            self.base = which * (N_KV_BLOCKS + 1)
            self.q = q_ref[rows_of(which), :]
            self.out_rows = rows_of(which)
            lhs = jnp.concatenate([k_ref[rows_of(blk_idx), :], *km_parts], axis=0)
            res = lax.dot_general(lhs, self.q, _NT, preferred_element_type=F32)
            nb = N_KV_BLOCKS
            bs = (res[MOBA_BLOCK:MOBA_BLOCK + nb] + res[MOBA_BLOCK + nb:MOBA_BLOCK + 2 * nb]
                  + res[MOBA_BLOCK + 2 * nb:MOBA_BLOCK + 3 * nb])
            blk = lax.broadcasted_iota(jnp.int32, bs.shape, 0)
            past = blk < blk_idx
            bs = jnp.where(past, bs, NEG_INF)
            rank = jnp.zeros(bs.shape, F32)
            for m in range(N_KV_BLOCKS):
                row = bs[m:m + 1, :]
                tie = jnp.where(blk > m, 1.0, 0.0)
                rank = rank + jnp.where(row > bs, 1.0, jnp.where(row == bs, tie, 0.0))
            self.bias = jnp.where(past, jnp.where(rank < MOBA_TOPK, 0.0, NEG_INF), NEG_INF)
            s = res[0:MOBA_BLOCK]
            ki = lax.broadcasted_iota(jnp.int32, s.shape, 0)
            qi = lax.broadcasted_iota(jnp.int32, s.shape, 1)
            s = jnp.where(ki <= qi, s, NEG_INF)
            s_scr[rows_of(self.base + N_KV_BLOCKS), :] = s
            self.m8 = jnp.max(fold8(s), axis=0)
            self.acc = [None, None]

        def pass1_tile(self, n):
            sn = lax.dot_general(k_ref[rows_of(n), :], self.q, _NT, preferred_element_type=F32)
            s_scr[rows_of(self.base + n), :] = sn
            self.m8 = jnp.maximum(self.m8, jnp.max(fold8(sn), axis=0) + self.bias[n:n + 1, :])

        def finish_pass1(self):
            self.m = jnp.max(self.m8, axis=0, keepdims=True)
            self._accumulate(self.idx, N_KV_BLOCKS, self.m)

        def _accumulate(self, v_blk, slab, shift):
            p = jnp.exp2(s_scr[rows_of(self.base + slab), :] - shift)
            pv = jnp.dot(vt_scr[v_blk], p.astype(BF16), preferred_element_type=F32)
            k = v_blk % 2
            self.acc[k] = pv if self.acc[k] is None else self.acc[k] + pv

        def pass2_tile(self, n):
            self._accumulate(n, n, self.m - self.bias[n:n + 1, :])

        def store(self):
            acc = self.acc[0] if self.acc[1] is None else (
                self.acc[1] if self.acc[0] is None else self.acc[0] + self.acc[1])
            l = acc[HEAD_DIM:HEAD_DIM + 1, :]
            o_ref[self.out_rows, :] = (acc[0:HEAD_DIM, :] / l).T.astype(BF16)

    def attend_pair(pair):
        km = kmean_scr[...]
        km_hi = km.astype(BF16)
        rest = km - km_hi.astype(F32)
        km_mid = rest.astype(BF16)
        km_lo = (rest - km_mid.astype(F32)).astype(BF16)
        parts = (km_hi, km_mid, km_lo)
        a = QueryBlock(0, 2 * pair, parts)
        for n in range(a.idx):
            a.pass1_tile(n)
        a.finish_pass1()
        b = QueryBlock(1, 2 * pair + 1, parts)
        for n in range(b.idx):
            b.pass1_tile(n)
            if n < a.idx:
                a.pass2_tile(n)
        a.store()
        b.finish_pass1()
        for n in range(b.idx):
            b.pass2_tile(n)
        b.store()

    for pair in range(N_KV_BLOCKS // 2):
        pl.when(i == pair)(functools.partial(attend_pair, pair))


def _attention(proj):
    pairs = N_KV_BLOCKS // 2
    return pl.pallas_call(
        _attn_kernel,
        grid=(BATCH, N_HEADS, pairs),
        in_specs=[
            pl.BlockSpec((2 * MOBA_BLOCK, HEAD_DIM), lambda b, h, i: (b * pairs + i, h)),
            pl.BlockSpec((SEQ, HEAD_DIM), lambda b, h, i: (b, COL_K + h)),
            pl.BlockSpec((SEQ, HEAD_DIM), lambda b, h, i: (b, COL_V + h)),
        ],
        out_specs=pl.BlockSpec((2 * MOBA_BLOCK, HEAD_DIM), lambda b, h, i: (b * pairs + i, h)),
        out_shape=jax.ShapeDtypeStruct((TOKENS, ATTN_WIDTH), BF16),
        scratch_shapes=[pltpu.VMEM((N_KV_BLOCKS, HEAD_DIM), F32),
                        pltpu.VMEM((N_KV_BLOCKS, HEAD_DIM + ATTN_ONES_ROWS, MOBA_BLOCK), BF16),
                        pltpu.VMEM((2 * (N_KV_BLOCKS + 1) * MOBA_BLOCK, MOBA_BLOCK), F32)],
        compiler_params=_params("parallel", "parallel", "arbitrary"),
        name="moba_attn",
    )(proj, proj, proj)


def _s5_prep_kernel(lr_r, li_r, ldt_r, btr_ref, bti_ref, ctr_ref, cti_ref,
                    tm_ref, wst_ref, wout_ref, al_ref):
    L = S5_CHUNK
    ns = S5_TILE_STATES
    lr = lr_r[...]
    li = li_r[...]
    dt = jnp.exp(ldt_r[...])
    mag = jnp.exp(lr * dt)
    ab_re = mag * jnp.cos(li * dt)
    ab_im = mag * jnp.sin(li * dt)
    den = lr * lr + li * li
    nr = ab_re - 1.0
    ni = ab_im
    coef_re = (nr * lr + ni * li) / den
    coef_im = (ni * lr - nr * li) / den

    p_re = [jnp.ones_like(ab_re)]
    p_im = [jnp.zeros_like(ab_re)]
    for _ in range(L):
        p_re.append(p_re[-1] * ab_re - p_im[-1] * ab_im)
        p_im.append(p_re[-2] * ab_im + p_im[-1] * ab_re)

    btr = btr_ref[...]
    bti = bti_ref[...]
    ctr = ctr_ref[...]
    cti = cti_ref[...]

    ctr_hl = _split_bf16(ctr)
    cti_hl = _split_bf16(cti)
    tm_ref[...] = jnp.zeros(tm_ref.shape, tm_ref.dtype)
    for t in range(L):
        wr = coef_re * p_re[t] - coef_im * p_im[t]
        wi = coef_re * p_im[t] + coef_im * p_re[t]
        s_re = btr * wr - bti * wi
        s_im = btr * wi + bti * wr
        rp = L - 1 - t
        wst_ref[rp * LANES:(rp + 1) * LANES, 0:ns] = s_re.astype(BF16)
        wst_ref[rp * LANES:(rp + 1) * LANES, ns:2 * ns] = s_im.astype(BF16)
        kt = _dot_split(s_re, ctr_hl) - _dot_split(s_im, cti_hl)
        ktb = kt.astype(BF16)
        for r0 in range(L - t):
            tm_ref[r0 * LANES:(r0 + 1) * LANES, (r0 + t) * LANES:(r0 + t + 1) * LANES] = ktb

    al_ref[:, 0:ns] = p_re[L]
    al_ref[:, ns:2 * ns] = p_im[L]

    pad = jnp.zeros((LANES - L, ns), F32)
    a_re = jnp.concatenate(p_re[1:] + [pad], axis=0).T
    a_im = jnp.concatenate(p_im[1:] + [pad], axis=0).T
    for r in range(L):
        ar = a_re[:, r:r + 1]
        ai = a_im[:, r:r + 1]
        wout_ref[0:ns, r * LANES:(r + 1) * LANES] = (ctr * ar - cti * ai).astype(BF16)
        wout_ref[ns:2 * ns, r * LANES:(r + 1) * LANES] = (-(ctr * ai + cti * ar)).astype(BF16)


def _s5_prep(lr_r, li_r, ldt_r, btr, bti, ctr, cti):
    ns = S5_TILE_STATES
    row = pl.BlockSpec((None, 1, ns), lambda t: (t, 0, 0))
    bt = pl.BlockSpec((None, LANES, ns), lambda t: (t, 0, 0))
    ct = pl.BlockSpec((None, ns, LANES), lambda t: (t, 0, 0))
    return pl.pallas_call(
        _s5_prep_kernel,
        grid=(S5_TILES,),
        in_specs=[row, row, row, bt, bt, ct, ct],
        out_specs=[pl.BlockSpec((None, S5_FLAT, S5_FLAT), lambda t: (t, 0, 0)),
                   pl.BlockSpec((None, S5_FLAT, 2 * ns), lambda t: (t, 0, 0)),
                   pl.BlockSpec((None, 2 * ns, S5_FLAT), lambda t: (t, 0, 0)),
                   pl.BlockSpec((None, 1, 2 * ns), lambda t: (t, 0, 0))],
        out_shape=[jax.ShapeDtypeStruct((S5_TILES, S5_FLAT, S5_FLAT), BF16),
                   jax.ShapeDtypeStruct((S5_TILES, S5_FLAT, 2 * ns), BF16),
                   jax.ShapeDtypeStruct((S5_TILES, 2 * ns, S5_FLAT), BF16),
                   jax.ShapeDtypeStruct((S5_TILES, 1, 2 * ns), F32)],
        compiler_params=_params("parallel"),
        name="s5_prep",
    )(lr_r, li_r, ldt_r, btr, bti, ctr, cti)


def _s5_kernel(u_ref, tm_ref, wst_ref, wout_ref, al_ref, d_ref, o_ref,
               uf_scr, uflat_scr, xs_scr, xp_scr, y_scr):
    L = S5_CHUNK
    ns = S5_TILE_STATES
    uf_scr[...] = u_ref[...].astype(F32)
    for r in range(L):
        uflat_scr[:, r * LANES:(r + 1) * LANES] = uf_scr[pl.ds(r, S5_CHUNKS, stride=L), :].astype(BF16)
    uflat = uflat_scr[...]

    xs_scr[...] = jnp.dot(uflat, wst_ref[...], preferred_element_type=F32)
    al = al_ref[...]
    al_re = al[:, 0:ns]
    al_im = al[:, ns:2 * ns]

    def body(c, state):
        s_re, s_im = state
        xp_scr[pl.ds(c, 1), 0:ns] = s_re
        xp_scr[pl.ds(c, 1), ns:2 * ns] = s_im
        x_re = xs_scr[pl.ds(c, 1), 0:ns]
        x_im = xs_scr[pl.ds(c, 1), ns:2 * ns]
        return (al_re * s_re - al_im * s_im + x_re, al_re * s_im + al_im * s_re + x_im)

    zero = jnp.zeros((1, ns), F32)
    lax.fori_loop(0, S5_CHUNKS, body, (zero, zero))

    xp = xp_scr[...].astype(BF16)
    step = S5_FLAT // S5_TRI_SPLITS
    for c0 in range(0, S5_FLAT, step):
        c1 = c0 + step
        y_scr[:, c0:c1] = (
            jnp.dot(uflat_scr[:, 0:c1], tm_ref[0:c1, c0:c1], preferred_element_type=F32)
            + jnp.dot(xp, wout_ref[:, c0:c1], preferred_element_type=F32))
    d = d_ref[...]
    for r in range(L):
        ur = uf_scr[pl.ds(r, S5_CHUNKS, stride=L), :]
        yr = y_scr[:, r * LANES:(r + 1) * LANES] + d * ur
        uf_scr[pl.ds(r, S5_CHUNKS, stride=L), :] = jax.nn.gelu(yr, approximate=True)
    o_ref[...] = uf_scr[...].astype(BF16)


def _s5(proj, tmat, wst, wout, al, d_row):
    ns = S5_TILE_STATES
    return pl.pallas_call(
        _s5_kernel,
        grid=(S5_TILES, BATCH),
        in_specs=[
            pl.BlockSpec((SEQ, LANES), lambda t, b: (b, COL_U + t)),
            pl.BlockSpec((None, S5_FLAT, S5_FLAT), lambda t, b: (t, 0, 0)),
            pl.BlockSpec((None, S5_FLAT, 2 * ns), lambda t, b: (t, 0, 0)),
            pl.BlockSpec((None, 2 * ns, S5_FLAT), lambda t, b: (t, 0, 0)),
            pl.BlockSpec((None, 1, 2 * ns), lambda t, b: (t, 0, 0)),
            pl.BlockSpec((1, LANES), lambda t, b: (0, t)),
        ],
        out_specs=pl.BlockSpec((SEQ, LANES), lambda t, b: (b, t)),
        out_shape=jax.ShapeDtypeStruct((TOKENS, SSM_WIDTH), BF16),
        scratch_shapes=[pltpu.VMEM((SEQ, LANES), F32),
                        pltpu.VMEM((S5_CHUNKS, S5_FLAT), BF16),
                        pltpu.VMEM((S5_CHUNKS, 2 * ns), F32),
                        pltpu.VMEM((S5_CHUNKS, 2 * ns), F32),
                        pltpu.VMEM((S5_CHUNKS, S5_FLAT), F32)],
        compiler_params=_params("parallel", "parallel"),
        name="s5_scan",
    )(proj, tmat, wst, wout, al, d_row)


def _merge_kernel(x_ref, gl_ref, ya_ref, ga_ref, gs_ref, g1_ref, wglu_ref, bglu_ref,
                  wa_ref, ws_ref, wo_ref, o_ref):
    gl = gl_ref[...]
    z = jnp.dot(gl, wglu_ref[...], preferred_element_type=F32) + bglu_ref[...]
    y_ssm = (gl.astype(F32) * jax.nn.sigmoid(z)).astype(BF16)
    a = jnp.dot(ya_ref[...], wa_ref[...], preferred_element_type=F32)
    s = jnp.dot(y_ssm, ws_ref[...], preferred_element_type=F32)
    merged = (jax.nn.sigmoid(ga_ref[...].astype(F32)) * a
              + jax.nn.sigmoid(gs_ref[...].astype(F32)) * s)
    out = jnp.dot(merged.astype(BF16), wo_ref[...], preferred_element_type=F32)
    o_ref[...] = x_ref[...] + g1_ref[...] * out


def _merge(x2, gl, y_attn, proj, mod3, w_glu, b_glu, w_attn_br, w_ssm_br, w_out):
    tm = TM_MERGE
    tiles_per_seq = SEQ // tm
    gate_blk = (3 * ATTN_WIDTH + SSM_WIDTH) // D_MODEL
    const = lambda shape: pl.BlockSpec(shape, lambda i: (0, 0), pipeline_mode=pl.Buffered(1))
    return pl.pallas_call(
        _merge_kernel,
        grid=(TOKENS // tm,),
        in_specs=[
            pl.BlockSpec((tm, D_MODEL), lambda i: (i, 0)),
            pl.BlockSpec((tm, SSM_WIDTH), lambda i: (i, 0)),
            pl.BlockSpec((tm, ATTN_WIDTH), lambda i: (i, 0)),
            pl.BlockSpec((tm, D_MODEL), lambda i: (i, gate_blk)),
            pl.BlockSpec((tm, D_MODEL), lambda i: (i, gate_blk + 1)),
            pl.BlockSpec((None, 1, D_MODEL), lambda i: (i // tiles_per_seq * N_MOD + 2, 0, 0)),
            const((SSM_WIDTH, SSM_WIDTH)),
            const((1, SSM_WIDTH)),
            const((ATTN_WIDTH, D_MODEL)),
            const((SSM_WIDTH, D_MODEL)),
            const((D_MODEL, D_MODEL)),
        ],
        out_specs=pl.BlockSpec((tm, D_MODEL), lambda i: (i, 0)),
        out_shape=jax.ShapeDtypeStruct((TOKENS, D_MODEL), F32),
        compiler_params=_params("parallel"),
        name="merge",
    )(x2, gl, y_attn, proj, proj, mod3, w_glu, b_glu, w_attn_br, w_ssm_br, w_out)


def _ffn_kernel(x_ref, halo_ref, sc_ref, sh_ref, g2_ref, ng_ref, wv_ref, wg_ref,
                cwv_ref, cwg_ref, cbv_ref, cbg_ref, wd_ref, o_ref, h_scr, acc_scr):
    i = pl.program_id(0)
    j = pl.program_id(1)
    tm = TM_FFN

    def conv(h, w_ref, cw_ref, cb_ref):
        up = jnp.dot(h, w_ref[...], preferred_element_type=F32)
        cw = cw_ref[...]
        up1 = pltpu.roll(up, 1, axis=0)
        up2 = pltpu.roll(up, 2, axis=0)
        y = cw[2:3, :] * up + cw[1:2, :] * up1 + cw[0:1, :] * up2 + cb_ref[...]
        return y[FFN_HALO:, :]

    def contribution():
        h = h_scr[...]
        val = conv(h, wv_ref, cwv_ref, cbv_ref)
        gt = conv(h, wg_ref, cwg_ref, cbg_ref)
        act = (gt * jax.nn.sigmoid(gt) * val).astype(BF16)
        return jnp.dot(act, wd_ref[...], preferred_element_type=F32)

    @pl.when(j == 0)
    def _():
        g = ng_ref[...]
        sc = sc_ref[...]
        sh = sh_ref[...]
        h_scr[FFN_HALO:, :] = _rms_modulate(x_ref[...], g, sc, sh).astype(BF16)
        halo = _rms_modulate(halo_ref[...], g, sc, sh)
        seq_start = (i % (SEQ // tm)) == 0
        h_scr[0:FFN_HALO, :] = jnp.where(seq_start, 0.0, halo).astype(BF16)
        acc_scr[...] = contribution()

    @pl.when((j > 0) & (j < N_F_TILES - 1))
    def _():
        acc_scr[...] += contribution()

    @pl.when(j == N_F_TILES - 1)
    def _():
        o_ref[...] = x_ref[...] + g2_ref[...] * (acc_scr[...] + contribution())


def _ffn(x1, mod3, norm2_g, w_up, conv_w, conv_b, w_down):
    tm, tf = TM_FFN, TF_FFN
    tiles_per_seq = SEQ // tm
    halo_per_tile = tm // FFN_HALO
    modspec = lambda k: pl.BlockSpec((None, 1, D_MODEL),
                                     lambda i, j: (i // tiles_per_seq * N_MOD + k, 0, 0))
    return pl.pallas_call(
        _ffn_kernel,
        grid=(TOKENS // tm, N_F_TILES),
        in_specs=[
            pl.BlockSpec((tm, D_MODEL), lambda i, j: (i, 0)),
            pl.BlockSpec((FFN_HALO, D_MODEL), lambda i, j: (jnp.maximum(i * halo_per_tile - 1, 0), 0)),
            modspec(4), modspec(3), modspec(5),
            pl.BlockSpec((1, D_MODEL), lambda i, j: (0, 0)),
            pl.BlockSpec((D_MODEL, tf), lambda i, j: (0, j)),
            pl.BlockSpec((D_MODEL, tf), lambda i, j: (0, N_F_TILES + j)),
            pl.BlockSpec((3, tf), lambda i, j: (0, j)),
            pl.BlockSpec((3, tf), lambda i, j: (0, N_F_TILES + j)),
            pl.BlockSpec((1, tf), lambda i, j: (0, j)),
            pl.BlockSpec((1, tf), lambda i, j: (0, N_F_TILES + j)),
            pl.BlockSpec((tf, D_MODEL), lambda i, j: (j, 0)),
        ],
        out_specs=pl.BlockSpec((tm, D_MODEL), lambda i, j: (i, 0)),
        out_shape=jax.ShapeDtypeStruct((TOKENS, D_MODEL), F32),
        scratch_shapes=[pltpu.VMEM((FFN_HALO + tm, D_MODEL), BF16),
                        pltpu.VMEM((tm, D_MODEL), F32)],
        compiler_params=_params("parallel", "arbitrary"),
        name="ffn",
    )(x1, x1, mod3, mod3, mod3, norm2_g, w_up, w_up, conv_w, conv_w, conv_b, conv_b, w_down)


def _block_diag(w):
    _, r, c = w.shape
    g = S5_TILE_GROUPS
    w = w.reshape(S5_TILES, g, r, 1, c)
    eye = jnp.eye(g, dtype=bool).reshape(1, g, 1, g, 1)
    return jnp.where(eye, w, 0.0).reshape(S5_TILES, g * r, g * c)


def _s5_branch(proj, lam_re, lam_im, log_dt, b_re, b_im, c_re, c_im, d_skip):
    ns = S5_TILE_STATES
    ldt = jnp.repeat(log_dt, SSM_STATE)
    rows = [a.reshape(S5_TILES, 1, ns) for a in (lam_re, lam_im, ldt)]
    btr = _block_diag(b_re.transpose(0, 2, 1))
    bti = _block_diag(b_im.transpose(0, 2, 1))
    ctr = _block_diag(c_re.transpose(0, 2, 1))
    cti = _block_diag(c_im.transpose(0, 2, 1))
    tmat, wst, wout, al = _s5_prep(*rows, btr, bti, ctr, cti)
    return _s5(proj, tmat, wst, wout, al, d_skip.reshape(1, -1))


def _layer(x2, c, pos_col, w_mod, b_mod, norm1_g, w_in, q_norm_g, k_norm_g,
           lam_re, lam_im, log_dt, b_re, b_im, c_re, c_im, d_skip, w_glu, b_glu,
           w_attn_br, w_ssm_br, w_out, norm2_g, w_up, conv_w, conv_b, w_down):
    c_pad = jnp.pad(c, ((0, SUBLANES - BATCH), (0, 0)))
    mod = _mod(c_pad, w_mod, b_mod.reshape(1, -1))[:BATCH]
    mod3 = mod.reshape(BATCH * N_MOD, 1, D_MODEL)

    half = HEAD_DIM // 2
    inv_freq = ROPE_THETA ** (-jnp.arange(half, dtype=F32) / half)
    freq_row = jnp.concatenate([inv_freq, inv_freq]).reshape(1, HEAD_DIM)
    cos, sin = _rope_tables(pos_col, freq_row)

    proj = _in_proj(x2, mod3, norm1_g.reshape(1, -1), w_in.astype(BF16), cos, sin,
                    q_norm_g.reshape(1, -1), k_norm_g.reshape(1, -1))
    y_attn = _attention(proj)

    gl = _s5_branch(proj, lam_re, lam_im, log_dt, b_re, b_im, c_re, c_im, d_skip)

    x1 = _merge(x2, gl, y_attn, proj, mod3, w_glu.astype(BF16), b_glu.reshape(1, -1),
                w_attn_br.astype(BF16), w_ssm_br.astype(BF16), w_out.astype(BF16))
    return _ffn(x1, mod3, norm2_g.reshape(1, -1), w_up.astype(BF16), conv_w,
                conv_b.reshape(1, -1), w_down.astype(BF16))


def kernel(x, c, positions, w_mod, b_mod, norm1_g, w_in, q_norm_g, k_norm_g, ssm_lambda_re, ssm_lambda_im, ssm_log_dt, ssm_b_re, ssm_b_im, ssm_c_re, ssm_c_im, ssm_d, w_glu, b_glu, w_attn_br, w_ssm_br, w_out, norm2_g, w_up, conv_w, conv_b, w_down):
    assert x.shape == (BATCH, SEQ, D_MODEL) and w_in.shape[0] == 1
    x2 = x.reshape(TOKENS, D_MODEL)
    pos_col = positions.reshape(TOKENS, 1)
    for l in range(w_in.shape[0]):
        x2 = _layer(x2, c, pos_col, w_mod[l], b_mod[l], norm1_g[l], w_in[l], q_norm_g[l], k_norm_g[l],
                    ssm_lambda_re[l], ssm_lambda_im[l], ssm_log_dt[l], ssm_b_re[l], ssm_b_im[l],
                    ssm_c_re[l], ssm_c_im[l], ssm_d[l], w_glu[l], b_glu[l], w_attn_br[l],
                    w_ssm_br[l], w_out[l], norm2_g[l], w_up[l], conv_w[l], conv_b[l], w_down[l])
    return x2.reshape(BATCH, SEQ, D_MODEL)
```

```python
import functools
import math

import jax
import jax.numpy as jnp
from jax import lax
from jax.experimental import pallas as pl
from jax.experimental.pallas import tpu as pltpu

F32 = jnp.float32
BF16 = jnp.bfloat16

D_MODEL = 2048
BATCH = 4
SEQ = 4096
TOKENS = BATCH * SEQ
HEAD_DIM = 128
N_HEADS = 8
ATTN_WIDTH = N_HEADS * HEAD_DIM
MOBA_BLOCK = 256
N_KV_BLOCKS = SEQ // MOBA_BLOCK
MOBA_TOPK = 3
ROPE_THETA = 10000.0
SSM_WIDTH = D_MODEL // 2
SSM_GROUP = 16
SSM_GROUPS = SSM_WIDTH // SSM_GROUP
SSM_STATE = 64
FFN_HIDDEN = 5632
N_MOD = 6
IN_WIDTH = 3 * ATTN_WIDTH + SSM_WIDTH + 2 * D_MODEL
EPS = 1e-6
NEG_INF = -1e30

LANES = 128
SUBLANES = 8
VMEM_LIMIT = 56 * 1024 * 1024

TM_PROJ = 512
TN_PROJ = 1024
ATTN_ONES_ROWS = 16
Q_SCALE = HEAD_DIM ** -0.5 * math.log2(math.e)
TM_MERGE = 256
TM_FFN = 512
TF_FFN = 512
N_F_TILES = FFN_HIDDEN // TF_FFN
FFN_TAIL = SUBLANES
S5_TILE_GROUPS = LANES // SSM_GROUP
S5_TILES = SSM_GROUPS // S5_TILE_GROUPS
S5_TILE_STATES = S5_TILE_GROUPS * SSM_STATE
S5_CHUNK = 16
S5_CHUNKS = SEQ // S5_CHUNK
S5_FLAT = S5_CHUNK * LANES
S5_TRI_SPLITS = 4

COL_K = ATTN_WIDTH // LANES
COL_V = 2 * ATTN_WIDTH // LANES
COL_U = 3 * ATTN_WIDTH // LANES


def _params(*sem):
    return pltpu.CompilerParams(dimension_semantics=sem, vmem_limit_bytes=VMEM_LIMIT)


def _split_bf16(x):
    hi = x.astype(BF16)
    return hi, (x - hi.astype(F32)).astype(BF16)


def _dot_split(a, b_hl):
    a_hi, a_lo = _split_bf16(a)
    b_hi, b_lo = b_hl
    dot = functools.partial(jnp.dot, preferred_element_type=F32)
    return dot(a_hi, b_hi) + (dot(a_hi, b_lo) + dot(a_lo, b_hi))


def _mod_kernel(c_ref, w_ref, b_ref, o_ref):
    c = c_ref[...]
    sc = c * jax.nn.sigmoid(c)
    o_ref[...] = _dot_split(sc, _split_bf16(w_ref[...])) + b_ref[...]


def _mod(c_pad, w_mod, b_mod):
    tn = 1024
    n = N_MOD * D_MODEL
    return pl.pallas_call(
        _mod_kernel,
        grid=(n // tn,),
        in_specs=[pl.BlockSpec((SUBLANES, D_MODEL), lambda j: (0, 0)),
                  pl.BlockSpec((D_MODEL, tn), lambda j: (0, j)),
                  pl.BlockSpec((1, tn), lambda j: (0, j))],
        out_specs=pl.BlockSpec((SUBLANES, tn), lambda j: (0, j)),
        out_shape=jax.ShapeDtypeStruct((SUBLANES, n), F32),
        compiler_params=_params("parallel"),
        name="mod",
    )(c_pad, w_mod, b_mod)


def _rope_kernel(pos_ref, freq_ref, cos_ref, sin_ref):
    ang = pos_ref[...].astype(F32) * freq_ref[...]
    lane = lax.broadcasted_iota(jnp.int32, ang.shape, 1)
    cos_ref[...] = jnp.cos(ang)
    sin_ref[...] = jnp.where(lane < HEAD_DIM // 2, -1.0, 1.0) * jnp.sin(ang)


def _rope_tables(pos_col, freq_row):
    tr = 2048
    return pl.pallas_call(
        _rope_kernel,
        grid=(TOKENS // tr,),
        in_specs=[pl.BlockSpec((tr, 1), lambda i: (i, 0)),
                  pl.BlockSpec((1, HEAD_DIM), lambda i: (0, 0))],
        out_specs=[pl.BlockSpec((tr, HEAD_DIM), lambda i: (i, 0)),
                   pl.BlockSpec((tr, HEAD_DIM), lambda i: (i, 0))],
        out_shape=[jax.ShapeDtypeStruct((TOKENS, HEAD_DIM), F32)] * 2,
        compiler_params=_params("parallel"),
        name="rope",
    )(pos_col, freq_row)


def _rms_modulate(x, g, scale, shift):
    ms = jnp.mean(x * x, axis=-1, keepdims=True)
    y = x * lax.rsqrt(ms + EPS) * g
    return y * (1.0 + scale) + shift


def _in_proj_kernel(x_ref, sc_ref, sh_ref, g_ref, w_ref, cos_ref, sin_ref, qg_ref, kg_ref,
                    o_ref, h_scr, raw_scr):
    j = pl.program_id(1)

    def project():
        return jnp.dot(h_scr[...], w_ref[...], preferred_element_type=F32)

    @pl.when(j == 0)
    def _():
        h_scr[...] = _rms_modulate(x_ref[...], g_ref[...], sc_ref[...], sh_ref[...]).astype(BF16)
        raw_scr[...] = project()

    @pl.when(j == 1)
    def _():
        raw_scr[...] = project()

    @pl.when(j < 2)
    def _():
        half = HEAD_DIM // 2
        g = jnp.where(j == 0, qg_ref[...], kg_ref[...])
        out_scale = jnp.where(j == 0, Q_SCALE, 1.0)
        c_tab = cos_ref[...] * (g * out_scale)
        s_tab = sin_ref[...] * (pltpu.roll(g, half, axis=1) * out_scale)
        for hd in range(N_HEADS):
            cols = slice(hd * HEAD_DIM, (hd + 1) * HEAD_DIM)
            t = raw_scr[:, cols]
            rs = lax.rsqrt(jnp.mean(t * t, axis=-1, keepdims=True) + EPS)
            o_ref[:, cols] = (rs * (t * c_tab + pltpu.roll(t, half, axis=1) * s_tab)).astype(BF16)

    @pl.when(j >= 2)
    def _():
        o_ref[...] = project().astype(BF16)


def _in_proj(x2, mod3, norm1_g, w_in_bf, cos, sin, qg, kg):
    tiles_per_seq = SEQ // TM_PROJ
    return pl.pallas_call(
        _in_proj_kernel,
        grid=(TOKENS // TM_PROJ, IN_WIDTH // TN_PROJ),
        in_specs=[
            pl.BlockSpec((TM_PROJ, D_MODEL), lambda i, j: (i, 0)),
            pl.BlockSpec((None, 1, D_MODEL), lambda i, j: (i // tiles_per_seq * N_MOD + 1, 0, 0)),
            pl.BlockSpec((None, 1, D_MODEL), lambda i, j: (i // tiles_per_seq * N_MOD + 0, 0, 0)),
            pl.BlockSpec((1, D_MODEL), lambda i, j: (0, 0)),
            pl.BlockSpec((D_MODEL, TN_PROJ), lambda i, j: (0, j)),
            pl.BlockSpec((TM_PROJ, HEAD_DIM), lambda i, j: (i, 0)),
            pl.BlockSpec((TM_PROJ, HEAD_DIM), lambda i, j: (i, 0)),
            pl.BlockSpec((1, HEAD_DIM), lambda i, j: (0, 0)),
            pl.BlockSpec((1, HEAD_DIM), lambda i, j: (0, 0)),
        ],
        out_specs=pl.BlockSpec((TM_PROJ, TN_PROJ), lambda i, j: (i, j)),
        out_shape=jax.ShapeDtypeStruct((TOKENS, IN_WIDTH), BF16),
        scratch_shapes=[pltpu.VMEM((TM_PROJ, D_MODEL), BF16),
                        pltpu.VMEM((TM_PROJ, TN_PROJ), F32)],
        compiler_params=_params("parallel", "arbitrary"),
        name="in_proj",
    )(x2, mod3, mod3, norm1_g, w_in_bf, cos, sin, qg, kg)


_NT = (((1,), (1,)), ((), ()))


def _attn_kernel(q_ref, k_ref, v_ref, o_ref, kmean_scr, vt_scr, s_scr):
    i = pl.program_id(2)

    @pl.when(i == 0)
    def _():
        for n in range(N_KV_BLOCKS):
            rows = slice(n * MOBA_BLOCK, (n + 1) * MOBA_BLOCK)
            kmean_scr[n:n + 1, :] = jnp.mean(k_ref[rows, :].astype(F32), axis=0, keepdims=True)
            vt_scr[n, 0:HEAD_DIM, :] = v_ref[rows, :].astype(F32).T.astype(BF16)
            vt_scr[n, HEAD_DIM:, :] = jnp.ones((ATTN_ONES_ROWS, MOBA_BLOCK), BF16)

    def fold8(t):
        return t.reshape(MOBA_BLOCK // SUBLANES, SUBLANES, MOBA_BLOCK)

    def rows_of(n):
        return slice(n * MOBA_BLOCK, (n + 1) * MOBA_BLOCK)

    class QueryBlock:
        def __init__(self, which, blk_idx, km_parts):
            self.idx = blk_idx
            self.base = which * (N_KV_BLOCKS + 1)
            self.q = q_ref[rows_of(which), :]
            self.out_rows = rows_of(which)
            lhs = jnp.concatenate([k_ref[rows_of(blk_idx), :], *km_parts], axis=0)
            res = lax.dot_general(lhs, self.q, _NT, preferred_element_type=F32)
            nb = N_KV_BLOCKS
            bs = (res[MOBA_BLOCK:MOBA_BLOCK + nb] + res[MOBA_BLOCK + nb:MOBA_BLOCK + 2 * nb]
                  + res[MOBA_BLOCK + 2 * nb:MOBA_BLOCK + 3 * nb])
            blk = lax.broadcasted_iota(jnp.int32, bs.shape, 0)
            past = blk < blk_idx
            bs = jnp.where(past, bs, NEG_INF)
            rank = jnp.zeros(bs.shape, F32)
            for m in range(N_KV_BLOCKS):
                row = bs[m:m + 1, :]
                tie = jnp.where(blk > m, 1.0, 0.0)
                rank = rank + jnp.where(row > bs, 1.0, jnp.where(row == bs, tie, 0.0))
            self.bias = jnp.where(past, jnp.where(rank < MOBA_TOPK, 0.0, NEG_INF), NEG_INF)
            s = res[0:MOBA_BLOCK]
            ki = lax.broadcasted_iota(jnp.int32, s.shape, 0)
            qi = lax.broadcasted_iota(jnp.int32, s.shape, 1)
            s = jnp.where(ki <= qi, s, NEG_INF)
            s_scr[rows_of(self.base + N_KV_BLOCKS), :] = s
            self.m8 = jnp.max(fold8(s), axis=0)
            self.acc = [None, None]

        def pass1_tile(self, n):
            sn = lax.dot_general(k_ref[rows_of(n), :], self.q, _NT, preferred_element_type=F32)
            s_scr[rows_of(self.base + n), :] = sn
            self.m8 = jnp.maximum(self.m8, jnp.max(fold8(sn), axis=0) + self.bias[n:n + 1, :])

        def finish_pass1(self):
            self.m = jnp.max(self.m8, axis=0, keepdims=True)
            self._accumulate(self.idx, N_KV_BLOCKS, self.m)

        def _accumulate(self, v_blk, slab, shift):
            p = jnp.exp2(s_scr[rows_of(self.base + slab), :] - shift)
            pv = jnp.dot(vt_scr[v_blk], p.astype(BF16), preferred_element_type=F32)
            k = v_blk % 2
            self.acc[k] = pv if self.acc[k] is None else self.acc[k] + pv

        def pass2_tile(self, n):
            self._accumulate(n, n, self.m - self.bias[n:n + 1, :])

        def store(self):
            acc = self.acc[0] if self.acc[1] is None else (
                self.acc[1] if self.acc[0] is None else self.acc[0] + self.acc[1])
            l = acc[HEAD_DIM:HEAD_DIM + 1, :]
            o_ref[self.out_rows, :] = (acc[0:HEAD_DIM, :] / l).T.astype(BF16)

    def attend_pair(pair):
        km = kmean_scr[...]
        km_hi = km.astype(BF16)
        rest = km - km_hi.astype(F32)
        km_mid = rest.astype(BF16)
        km_lo = (rest - km_mid.astype(F32)).astype(BF16)
        parts = (km_hi, km_mid, km_lo)
        a = QueryBlock(0, 2 * pair, parts)
        for n in range(a.idx):
            a.pass1_tile(n)
        a.finish_pass1()
        b = QueryBlock(1, 2 * pair + 1, parts)
        for n in range(b.idx):
            b.pass1_tile(n)
            if n < a.idx:
                a.pass2_tile(n)
        a.store()
        b.finish_pass1()
        for n in range(b.idx):
            b.pass2_tile(n)
        b.store()

    for pair in range(N_KV_BLOCKS // 2):
        pl.when(i == pair)(functools.partial(attend_pair, pair))


def _attention(proj):
    pairs = N_KV_BLOCKS // 2
    return pl.pallas_call(
        _attn_kernel,
        grid=(BATCH, N_HEADS, pairs),
        in_specs=[
            pl.BlockSpec((2 * MOBA_BLOCK, HEAD_DIM), lambda b, h, i: (b * pairs + i, h)),
            pl.BlockSpec((SEQ, HEAD_DIM), lambda b, h, i: (b, COL_K + h)),
            pl.BlockSpec((SEQ, HEAD_DIM), lambda b, h, i: (b, COL_V + h)),
        ],
        out_specs=pl.BlockSpec((2 * MOBA_BLOCK, HEAD_DIM), lambda b, h, i: (b * pairs + i, h)),
        out_shape=jax.ShapeDtypeStruct((TOKENS, ATTN_WIDTH), BF16),
        scratch_shapes=[pltpu.VMEM((N_KV_BLOCKS, HEAD_DIM), F32),
                        pltpu.VMEM((N_KV_BLOCKS, HEAD_DIM + ATTN_ONES_ROWS, MOBA_BLOCK), BF16),
                        pltpu.VMEM((2 * (N_KV_BLOCKS + 1) * MOBA_BLOCK, MOBA_BLOCK), F32)],
        compiler_params=_params("parallel", "parallel", "arbitrary"),
        name="moba_attn",
    )(proj, proj, proj)


def _s5_prep_kernel(lr_r, li_r, ldt_r, btr_ref, bti_ref, ctr_ref, cti_ref,
                    tm_ref, wst_ref, wout_ref, al_ref):
    L = S5_CHUNK
    ns = S5_TILE_STATES
    lr = lr_r[...]
    li = li_r[...]
    dt = jnp.exp(ldt_r[...])
    mag = jnp.exp(lr * dt)
    ab_re = mag * jnp.cos(li * dt)
    ab_im = mag * jnp.sin(li * dt)
    den = lr * lr + li * li
    nr = ab_re - 1.0
    ni = ab_im
    coef_re = (nr * lr + ni * li) / den
    coef_im = (ni * lr - nr * li) / den

    p_re = [jnp.ones_like(ab_re)]
    p_im = [jnp.zeros_like(ab_re)]
    for _ in range(L):
        p_re.append(p_re[-1] * ab_re - p_im[-1] * ab_im)
        p_im.append(p_re[-2] * ab_im + p_im[-1] * ab_re)

    btr = btr_ref[...]
    bti = bti_ref[...]
    ctr = ctr_ref[...]
    cti = cti_ref[...]

    ctr_hl = _split_bf16(ctr)
    cti_hl = _split_bf16(cti)
    tm_ref[...] = jnp.zeros(tm_ref.shape, tm_ref.dtype)
    for t in range(L):
        wr = coef_re * p_re[t] - coef_im * p_im[t]
        wi = coef_re * p_im[t] + coef_im * p_re[t]
        s_re = btr * wr - bti * wi
        s_im = btr * wi + bti * wr
        rp = L - 1 - t
        wst_ref[rp * LANES:(rp + 1) * LANES, 0:ns] = s_re.astype(BF16)
        wst_ref[rp * LANES:(rp + 1) * LANES, ns:2 * ns] = s_im.astype(BF16)
        kt = _dot_split(s_re, ctr_hl) - _dot_split(s_im, cti_hl)
        ktb = kt.astype(BF16)
        for r0 in range(L - t):
            tm_ref[r0 * LANES:(r0 + 1) * LANES, (r0 + t) * LANES:(r0 + t + 1) * LANES] = ktb

    al_ref[:, 0:ns] = p_re[L]
    al_ref[:, ns:2 * ns] = p_im[L]

    pad = jnp.zeros((LANES - L, ns), F32)
    a_re = jnp.concatenate(p_re[1:] + [pad], axis=0).T
    a_im = jnp.concatenate(p_im[1:] + [pad], axis=0).T
    for r in range(L):
        ar = a_re[:, r:r + 1]
        ai = a_im[:, r:r + 1]
        wout_ref[0:ns, r * LANES:(r + 1) * LANES] = (ctr * ar - cti * ai).astype(BF16)
        wout_ref[ns:2 * ns, r * LANES:(r + 1) * LANES] = (-(ctr * ai + cti * ar)).astype(BF16)


def _s5_prep(lr_r, li_r, ldt_r, btr, bti, ctr, cti):
    ns = S5_TILE_STATES
    row = pl.BlockSpec((None, 1, ns), lambda t: (t, 0, 0))
    bt = pl.BlockSpec((None, LANES, ns), lambda t: (t, 0, 0))
    ct = pl.BlockSpec((None, ns, LANES), lambda t: (t, 0, 0))
    return pl.pallas_call(
        _s5_prep_kernel,
        grid=(S5_TILES,),
        in_specs=[row, row, row, bt, bt, ct, ct],
        out_specs=[pl.BlockSpec((None, S5_FLAT, S5_FLAT), lambda t: (t, 0, 0)),
                   pl.BlockSpec((None, S5_FLAT, 2 * ns), lambda t: (t, 0, 0)),
                   pl.BlockSpec((None, 2 * ns, S5_FLAT), lambda t: (t, 0, 0)),
                   pl.BlockSpec((None, 1, 2 * ns), lambda t: (t, 0, 0))],
        out_shape=[jax.ShapeDtypeStruct((S5_TILES, S5_FLAT, S5_FLAT), BF16),
                   jax.ShapeDtypeStruct((S5_TILES, S5_FLAT, 2 * ns), BF16),
                   jax.ShapeDtypeStruct((S5_TILES, 2 * ns, S5_FLAT), BF16),
                   jax.ShapeDtypeStruct((S5_TILES, 1, 2 * ns), F32)],
        compiler_params=_params("parallel"),
        name="s5_prep",
    )(lr_r, li_r, ldt_r, btr, bti, ctr, cti)


def _s5_kernel(u_ref, tm_ref, wst_ref, wout_ref, al_ref, d_ref, o_ref,
               uf_scr, uflat_scr, xs_scr, xp_scr, y_scr):
    L = S5_CHUNK
    ns = S5_TILE_STATES
    uf_scr[...] = u_ref[...].astype(F32)
    for r in range(L):
        uflat_scr[:, r * LANES:(r + 1) * LANES] = uf_scr[pl.ds(r, S5_CHUNKS, stride=L), :].astype(BF16)
    uflat = uflat_scr[...]

    xs_scr[...] = jnp.dot(uflat, wst_ref[...], preferred_element_type=F32)
    al = al_ref[...]
    al_re = al[:, 0:ns]
    al_im = al[:, ns:2 * ns]

    def body(c, state):
        s_re, s_im = state
        xp_scr[pl.ds(c, 1), 0:ns] = s_re
        xp_scr[pl.ds(c, 1), ns:2 * ns] = s_im
        x_re = xs_scr[pl.ds(c, 1), 0:ns]
        x_im = xs_scr[pl.ds(c, 1), ns:2 * ns]
        return (al_re * s_re - al_im * s_im + x_re, al_re * s_im + al_im * s_re + x_im)

    zero = jnp.zeros((1, ns), F32)
    lax.fori_loop(0, S5_CHUNKS, body, (zero, zero))

    xp = xp_scr[...].astype(BF16)
    step = S5_FLAT // S5_TRI_SPLITS
    for c0 in range(0, S5_FLAT, step):
        c1 = c0 + step
        y_scr[:, c0:c1] = (
            jnp.dot(uflat_scr[:, 0:c1], tm_ref[0:c1, c0:c1], preferred_element_type=F32)
            + jnp.dot(xp, wout_ref[:, c0:c1], preferred_element_type=F32))
    d = d_ref[...]
    for r in range(L):
        ur = uf_scr[pl.ds(r, S5_CHUNKS, stride=L), :]
        yr = y_scr[:, r * LANES:(r + 1) * LANES] + d * ur
        uf_scr[pl.ds(r, S5_CHUNKS, stride=L), :] = jax.nn.gelu(yr, approximate=True)
    o_ref[...] = uf_scr[...].astype(BF16)


def _s5(proj, tmat, wst, wout, al, d_row):
    ns = S5_TILE_STATES
    return pl.pallas_call(
        _s5_kernel,
        grid=(S5_TILES, BATCH),
        in_specs=[
            pl.BlockSpec((SEQ, LANES), lambda t, b: (b, COL_U + t)),
            pl.BlockSpec((None, S5_FLAT, S5_FLAT), lambda t, b: (t, 0, 0)),
            pl.BlockSpec((None, S5_FLAT, 2 * ns), lambda t, b: (t, 0, 0)),
            pl.BlockSpec((None, 2 * ns, S5_FLAT), lambda t, b: (t, 0, 0)),
            pl.BlockSpec((None, 1, 2 * ns), lambda t, b: (t, 0, 0)),
            pl.BlockSpec((1, LANES), lambda t, b: (0, t)),
        ],
        out_specs=pl.BlockSpec((SEQ, LANES), lambda t, b: (b, t)),
        out_shape=jax.ShapeDtypeStruct((TOKENS, SSM_WIDTH), BF16),
        scratch_shapes=[pltpu.VMEM((SEQ, LANES), F32),
                        pltpu.VMEM((S5_CHUNKS, S5_FLAT), BF16),
                        pltpu.VMEM((S5_CHUNKS, 2 * ns), F32),
                        pltpu.VMEM((S5_CHUNKS, 2 * ns), F32),
                        pltpu.VMEM((S5_CHUNKS, S5_FLAT), F32)],
        compiler_params=_params("parallel", "parallel"),
        name="s5_scan",
    )(proj, tmat, wst, wout, al, d_row)


def _merge_kernel(x_ref, gl_ref, ya_ref, ga_ref, gs_ref, g1_ref, wglu_ref, bglu_ref,
                  wa_ref, ws_ref, wo_ref, o_ref):
    gl = gl_ref[...]
    z = jnp.dot(gl, wglu_ref[...], preferred_element_type=F32) + bglu_ref[...]
    y_ssm = (gl.astype(F32) * jax.nn.sigmoid(z)).astype(BF16)
    a = jnp.dot(ya_ref[...], wa_ref[...], preferred_element_type=F32)
    s = jnp.dot(y_ssm, ws_ref[...], preferred_element_type=F32)
    merged = (jax.nn.sigmoid(ga_ref[...].astype(F32)) * a
              + jax.nn.sigmoid(gs_ref[...].astype(F32)) * s)
    out = jnp.dot(merged.astype(BF16), wo_ref[...], preferred_element_type=F32)
    o_ref[...] = x_ref[...] + g1_ref[...] * out


def _merge(x2, gl, y_attn, proj, mod3, w_glu, b_glu, w_attn_br, w_ssm_br, w_out):
    tm = TM_MERGE
    tiles_per_seq = SEQ // tm
    gate_blk = (3 * ATTN_WIDTH + SSM_WIDTH) // D_MODEL
    const = lambda shape: pl.BlockSpec(shape, lambda i: (0, 0), pipeline_mode=pl.Buffered(1))
    return pl.pallas_call(
        _merge_kernel,
        grid=(TOKENS // tm,),
        in_specs=[
            pl.BlockSpec((tm, D_MODEL), lambda i: (i, 0)),
            pl.BlockSpec((tm, SSM_WIDTH), lambda i: (i, 0)),
            pl.BlockSpec((tm, ATTN_WIDTH), lambda i: (i, 0)),
            pl.BlockSpec((tm, D_MODEL), lambda i: (i, gate_blk)),
            pl.BlockSpec((tm, D_MODEL), lambda i: (i, gate_blk + 1)),
            pl.BlockSpec((None, 1, D_MODEL), lambda i: (i // tiles_per_seq * N_MOD + 2, 0, 0)),
            const((SSM_WIDTH, SSM_WIDTH)),
            const((1, SSM_WIDTH)),
            const((ATTN_WIDTH, D_MODEL)),
            const((SSM_WIDTH, D_MODEL)),
            const((D_MODEL, D_MODEL)),
        ],
        out_specs=pl.BlockSpec((tm, D_MODEL), lambda i: (i, 0)),
        out_shape=jax.ShapeDtypeStruct((TOKENS, D_MODEL), F32),
        compiler_params=_params("parallel"),
        name="merge",
    )(x2, gl, y_attn, proj, proj, mod3, w_glu, b_glu, w_attn_br, w_ssm_br, w_out)


def _ffn_kernel(x_ref, sc_ref, sh_ref, g2_ref, ng_ref, wv_ref, wg_ref,
                cwv_ref, cwg_ref, cbv_ref, cbg_ref, wd_ref, o_ref, h_scr, acc_scr, tail_scr):
    i = pl.program_id(0)
    j = pl.program_id(1)
    tm = TM_FFN

    @pl.when((i % (SEQ // tm)) == 0)
    def _():
        tail_scr[j] = jnp.zeros(tail_scr.shape[1:], F32)

    def conv(h, w_ref, cw_ref, cb_ref, which):
        up = jnp.dot(h, w_ref[...], preferred_element_type=F32)
        ext = jnp.concatenate([tail_scr[j, which], up], axis=0)
        tail_scr[j, which] = up[tm - FFN_TAIL:, :]
        cw = cw_ref[...]
        ext1 = pltpu.roll(ext, 1, axis=0)
        ext2 = pltpu.roll(ext, 2, axis=0)
        y = cw[2:3, :] * ext + cw[1:2, :] * ext1 + cw[0:1, :] * ext2 + cb_ref[...]
        return y[FFN_TAIL:, :]

    def contribution():
        h = h_scr[...]
        val = conv(h, wv_ref, cwv_ref, cbv_ref, 0)
        gt = conv(h, wg_ref, cwg_ref, cbg_ref, 1)
        act = (gt * jax.nn.sigmoid(gt) * val).astype(BF16)
        return jnp.dot(act, wd_ref[...], preferred_element_type=F32)

    @pl.when(j == 0)
    def _():
        h_scr[...] = _rms_modulate(x_ref[...], ng_ref[...], sc_ref[...], sh_ref[...]).astype(BF16)
        acc_scr[...] = contribution()

    @pl.when((j > 0) & (j < N_F_TILES - 1))
    def _():
        acc_scr[...] += contribution()

    @pl.when(j == N_F_TILES - 1)
    def _():
        o_ref[...] = x_ref[...] + g2_ref[...] * (acc_scr[...] + contribution())


def _ffn(x1, mod3, norm2_g, w_up, conv_w, conv_b, w_down):
    tm, tf = TM_FFN, TF_FFN
    tiles_per_seq = SEQ // tm
    modspec = lambda k: pl.BlockSpec((None, 1, D_MODEL),
                                     lambda i, j: (i // tiles_per_seq * N_MOD + k, 0, 0))
    return pl.pallas_call(
        _ffn_kernel,
        grid=(TOKENS // tm, N_F_TILES),
        in_specs=[
            pl.BlockSpec((tm, D_MODEL), lambda i, j: (i, 0)),
            modspec(4), modspec(3), modspec(5),
            pl.BlockSpec((1, D_MODEL), lambda i, j: (0, 0)),
            pl.BlockSpec((D_MODEL, tf), lambda i, j: (0, j)),
            pl.BlockSpec((D_MODEL, tf), lambda i, j: (0, N_F_TILES + j)),
            pl.BlockSpec((3, tf), lambda i, j: (0, j)),
            pl.BlockSpec((3, tf), lambda i, j: (0, N_F_TILES + j)),
            pl.BlockSpec((1, tf), lambda i, j: (0, j)),
            pl.BlockSpec((1, tf), lambda i, j: (0, N_F_TILES + j)),
            pl.BlockSpec((tf, D_MODEL), lambda i, j: (j, 0)),
        ],
        out_specs=pl.BlockSpec((tm, D_MODEL), lambda i, j: (i, 0)),
        out_shape=jax.ShapeDtypeStruct((TOKENS, D_MODEL), F32),
        scratch_shapes=[pltpu.VMEM((tm, D_MODEL), BF16),
                        pltpu.VMEM((tm, D_MODEL), F32),
                        pltpu.VMEM((N_F_TILES, 2, FFN_TAIL, tf), F32)],
        compiler_params=_params("arbitrary", "arbitrary"),
        name="ffn",
    )(x1, mod3, mod3, mod3, norm2_g, w_up, w_up, conv_w, conv_w, conv_b, conv_b, w_down)


def _block_diag(w):
    _, r, c = w.shape
    g = S5_TILE_GROUPS
    w = w.reshape(S5_TILES, g, r, 1, c)
    eye = jnp.eye(g, dtype=bool).reshape(1, g, 1, g, 1)
    return jnp.where(eye, w, 0.0).reshape(S5_TILES, g * r, g * c)


def _s5_branch(proj, lam_re, lam_im, log_dt, b_re, b_im, c_re, c_im, d_skip):
    ns = S5_TILE_STATES
    ldt = jnp.repeat(log_dt, SSM_STATE)
    rows = [a.reshape(S5_TILES, 1, ns) for a in (lam_re, lam_im, ldt)]
    btr = _block_diag(b_re.transpose(0, 2, 1))
    bti = _block_diag(b_im.transpose(0, 2, 1))
    ctr = _block_diag(c_re.transpose(0, 2, 1))
    cti = _block_diag(c_im.transpose(0, 2, 1))
    tmat, wst, wout, al = _s5_prep(*rows, btr, bti, ctr, cti)
    return _s5(proj, tmat, wst, wout, al, d_skip.reshape(1, -1))


def _layer(x2, c, pos_col, w_mod, b_mod, norm1_g, w_in, q_norm_g, k_norm_g,
           lam_re, lam_im, log_dt, b_re, b_im, c_re, c_im, d_skip, w_glu, b_glu,
           w_attn_br, w_ssm_br, w_out, norm2_g, w_up, conv_w, conv_b, w_down):
    c_pad = jnp.pad(c, ((0, SUBLANES - BATCH), (0, 0)))
    mod = _mod(c_pad, w_mod, b_mod.reshape(1, -1))[:BATCH]
    mod3 = mod.reshape(BATCH * N_MOD, 1, D_MODEL)

    half = HEAD_DIM // 2
    inv_freq = ROPE_THETA ** (-jnp.arange(half, dtype=F32) / half)
    freq_row = jnp.concatenate([inv_freq, inv_freq]).reshape(1, HEAD_DIM)
    cos, sin = _rope_tables(pos_col, freq_row)

    proj = _in_proj(x2, mod3, norm1_g.reshape(1, -1), w_in.astype(BF16), cos, sin,
                    q_norm_g.reshape(1, -1), k_norm_g.reshape(1, -1))
    y_attn = _attention(proj)

    gl = _s5_branch(proj, lam_re, lam_im, log_dt, b_re, b_im, c_re, c_im, d_skip)

    x1 = _merge(x2, gl, y_attn, proj, mod3, w_glu.astype(BF16), b_glu.reshape(1, -1),
                w_attn_br.astype(BF16), w_ssm_br.astype(BF16), w_out.astype(BF16))
    return _ffn(x1, mod3, norm2_g.reshape(1, -1), w_up.astype(BF16), conv_w,
                conv_b.reshape(1, -1), w_down.astype(BF16))


def kernel(x, c, positions, w_mod, b_mod, norm1_g, w_in, q_norm_g, k_norm_g, ssm_lambda_re, ssm_lambda_im, ssm_log_dt, ssm_b_re, ssm_b_im, ssm_c_re, ssm_c_im, ssm_d, w_glu, b_glu, w_attn_br, w_ssm_br, w_out, norm2_g, w_up, conv_w, conv_b, w_down):
    assert x.shape == (BATCH, SEQ, D_MODEL) and w_in.shape[0] == 1
    x2 = x.reshape(TOKENS, D_MODEL)
    pos_col = positions.reshape(TOKENS, 1)
    for l in range(w_in.shape[0]):
        x2 = _layer(x2, c, pos_col, w_mod[l], b_mod[l], norm1_g[l], w_in[l], q_norm_g[l], k_norm_g[l],
                    ssm_lambda_re[l], ssm_lambda_im[l], ssm_log_dt[l], ssm_b_re[l], ssm_b_im[l],
                    ssm_c_re[l], ssm_c_im[l], ssm_d[l], w_glu[l], b_glu[l], w_attn_br[l],
                    w_ssm_br[l], w_out[l], norm2_g[l], w_up[l], conv_w[l], conv_b[l], w_down[l])
    return x2.reshape(BATCH, SEQ, D_MODEL)
```

```python
import functools
import math

import jax
import jax.numpy as jnp
from jax import lax
from jax.experimental import pallas as pl
from jax.experimental.pallas import tpu as pltpu

F32 = jnp.float32
BF16 = jnp.bfloat16

D_MODEL = 2048
BATCH = 4
SEQ = 4096
TOKENS = BATCH * SEQ
HEAD_DIM = 128
N_HEADS = 8
ATTN_WIDTH = N_HEADS * HEAD_DIM
MOBA_BLOCK = 256
N_KV_BLOCKS = SEQ // MOBA_BLOCK
MOBA_TOPK = 3
ROPE_THETA = 10000.0
SSM_WIDTH = D_MODEL // 2
SSM_GROUP = 16
SSM_GROUPS = SSM_WIDTH // SSM_GROUP
SSM_STATE = 64
FFN_HIDDEN = 5632
N_MOD = 6
IN_WIDTH = 3 * ATTN_WIDTH + SSM_WIDTH + 2 * D_MODEL
EPS = 1e-6
NEG_INF = -1e30

LANES = 128
SUBLANES = 8
VMEM_LIMIT = 56 * 1024 * 1024

TM_PROJ = 512
TN_PROJ = 1024
ATTN_ONES_ROWS = 16
Q_SCALE = HEAD_DIM ** -0.5 * math.log2(math.e)
TM_MERGE = 256
TM_FFN = 512
TF_FFN = 512
N_F_TILES = FFN_HIDDEN // TF_FFN
FFN_TAIL = SUBLANES
S5_TILE_GROUPS = LANES // SSM_GROUP
S5_TILES = SSM_GROUPS // S5_TILE_GROUPS
S5_TILE_STATES = S5_TILE_GROUPS * SSM_STATE
S5_CHUNK = 16
S5_CHUNKS = SEQ // S5_CHUNK
S5_FLAT = S5_CHUNK * LANES
S5_TRI_SPLITS = 4

COL_K = ATTN_WIDTH // LANES
COL_V = 2 * ATTN_WIDTH // LANES
COL_U = 3 * ATTN_WIDTH // LANES


def _params(*sem):
    return pltpu.CompilerParams(dimension_semantics=sem, vmem_limit_bytes=VMEM_LIMIT)


def _split_bf16(x):
    hi = x.astype(BF16)
    return hi, (x - hi.astype(F32)).astype(BF16)


def _dot_split(a, b_hl):
    a_hi, a_lo = _split_bf16(a)
    b_hi, b_lo = b_hl
    dot = functools.partial(jnp.dot, preferred_element_type=F32)
    return dot(a_hi, b_hi) + (dot(a_hi, b_lo) + dot(a_lo, b_hi))


def _mod_kernel(c_ref, w_ref, b_ref, o_ref):
    c = c_ref[...]
    sc = c * jax.nn.sigmoid(c)
    o_ref[...] = _dot_split(sc, _split_bf16(w_ref[...])) + b_ref[...]


def _mod(c_pad, w_mod, b_mod):
    tn = 1024
    n = N_MOD * D_MODEL
    return pl.pallas_call(
        _mod_kernel,
        grid=(n // tn,),
        in_specs=[pl.BlockSpec((SUBLANES, D_MODEL), lambda j: (0, 0)),
                  pl.BlockSpec((D_MODEL, tn), lambda j: (0, j)),
                  pl.BlockSpec((1, tn), lambda j: (0, j))],
        out_specs=pl.BlockSpec((SUBLANES, tn), lambda j: (0, j)),
        out_shape=jax.ShapeDtypeStruct((SUBLANES, n), F32),
        compiler_params=_params("parallel"),
        name="mod",
    )(c_pad, w_mod, b_mod)


def _cast_slabs(refs):
    for src, dst in refs:
        dst[...] = src[...].astype(BF16)


def _slab_specs(weights, steps, index):
    specs, shapes = [], []
    for w in weights:
        rows, cols = w.shape
        slab = rows // steps
        assert slab * steps == rows and slab % (2 * SUBLANES) == 0, (w.shape, steps)
        specs.append(pl.BlockSpec((slab, cols), lambda *ids: (index(*ids), 0)))
        shapes.append(jax.ShapeDtypeStruct(w.shape, BF16))
    return specs, shapes


def _rope_kernel(pos_ref, freq_ref, w_ref, cos_ref, sin_ref, wbf_ref):
    ang = pos_ref[...].astype(F32) * freq_ref[...]
    lane = lax.broadcasted_iota(jnp.int32, ang.shape, 1)
    cos_ref[...] = jnp.cos(ang)
    sin_ref[...] = jnp.where(lane < HEAD_DIM // 2, -1.0, 1.0) * jnp.sin(ang)
    _cast_slabs([(w_ref, wbf_ref)])


def _rope_tables(pos_col, freq_row, w_in):
    tr = 2048
    steps = TOKENS // tr
    w_specs, w_shapes = _slab_specs([w_in], steps, lambda i: i)
    return pl.pallas_call(
        _rope_kernel,
        grid=(steps,),
        in_specs=[pl.BlockSpec((tr, 1), lambda i: (i, 0)),
                  pl.BlockSpec((1, HEAD_DIM), lambda i: (0, 0))] + w_specs,
        out_specs=[pl.BlockSpec((tr, HEAD_DIM), lambda i: (i, 0)),
                   pl.BlockSpec((tr, HEAD_DIM), lambda i: (i, 0))] + w_specs,
        out_shape=[jax.ShapeDtypeStruct((TOKENS, HEAD_DIM), F32)] * 2 + w_shapes,
        compiler_params=_params("parallel"),
        name="rope",
    )(pos_col, freq_row, w_in)


def _rms_modulate(x, g, scale, shift):
    ms = jnp.mean(x * x, axis=-1, keepdims=True)
    y = x * lax.rsqrt(ms + EPS) * g
    return y * (1.0 + scale) + shift


def _in_proj_kernel(x_ref, sc_ref, sh_ref, g_ref, w_ref, cos_ref, sin_ref, qg_ref, kg_ref,
                    o_ref, h_scr, raw_scr):
    j = pl.program_id(1)

    def project():
        return jnp.dot(h_scr[...], w_ref[...], preferred_element_type=F32)

    @pl.when(j == 0)
    def _():
        h_scr[...] = _rms_modulate(x_ref[...], g_ref[...], sc_ref[...], sh_ref[...]).astype(BF16)
        raw_scr[...] = project()

    @pl.when(j == 1)
    def _():
        raw_scr[...] = project()

    @pl.when(j < 2)
    def _():
        half = HEAD_DIM // 2
        g = jnp.where(j == 0, qg_ref[...], kg_ref[...])
        out_scale = jnp.where(j == 0, Q_SCALE, 1.0)
        c_tab = cos_ref[...] * (g * out_scale)
        s_tab = sin_ref[...] * (pltpu.roll(g, half, axis=1) * out_scale)
        for hd in range(N_HEADS):
            cols = slice(hd * HEAD_DIM, (hd + 1) * HEAD_DIM)
            t = raw_scr[:, cols]
            rs = lax.rsqrt(jnp.mean(t * t, axis=-1, keepdims=True) + EPS)
            o_ref[:, cols] = (rs * (t * c_tab + pltpu.roll(t, half, axis=1) * s_tab)).astype(BF16)

    @pl.when(j >= 2)
    def _():
        o_ref[...] = project().astype(BF16)


def _in_proj(x2, mod3, norm1_g, w_in_bf, cos, sin, qg, kg):
    tiles_per_seq = SEQ // TM_PROJ
    return pl.pallas_call(
        _in_proj_kernel,
        grid=(TOKENS // TM_PROJ, IN_WIDTH // TN_PROJ),
        in_specs=[
            pl.BlockSpec((TM_PROJ, D_MODEL), lambda i, j: (i, 0)),
            pl.BlockSpec((None, 1, D_MODEL), lambda i, j: (i // tiles_per_seq * N_MOD + 1, 0, 0)),
            pl.BlockSpec((None, 1, D_MODEL), lambda i, j: (i // tiles_per_seq * N_MOD + 0, 0, 0)),
            pl.BlockSpec((1, D_MODEL), lambda i, j: (0, 0)),
            pl.BlockSpec((D_MODEL, TN_PROJ), lambda i, j: (0, j)),
            pl.BlockSpec((TM_PROJ, HEAD_DIM), lambda i, j: (i, 0)),
            pl.BlockSpec((TM_PROJ, HEAD_DIM), lambda i, j: (i, 0)),
            pl.BlockSpec((1, HEAD_DIM), lambda i, j: (0, 0)),
            pl.BlockSpec((1, HEAD_DIM), lambda i, j: (0, 0)),
        ],
        out_specs=pl.BlockSpec((TM_PROJ, TN_PROJ), lambda i, j: (i, j)),
        out_shape=jax.ShapeDtypeStruct((TOKENS, IN_WIDTH), BF16),
        scratch_shapes=[pltpu.VMEM((TM_PROJ, D_MODEL), BF16),
                        pltpu.VMEM((TM_PROJ, TN_PROJ), F32)],
        compiler_params=_params("parallel", "arbitrary"),
        name="in_proj",
    )(x2, mod3, mod3, norm1_g, w_in_bf, cos, sin, qg, kg)


_NT = (((1,), (1,)), ((), ()))


def _attn_kernel(q_ref, k_ref, v_ref, o_ref, kmean_scr, vt_scr, s_scr):
    i = pl.program_id(2)

    @pl.when(i == 0)
    def _():
        for n in range(N_KV_BLOCKS):
            rows = slice(n * MOBA_BLOCK, (n + 1) * MOBA_BLOCK)
            kmean_scr[n:n + 1, :] = jnp.mean(k_ref[rows, :].astype(F32), axis=0, keepdims=True)
            vt_scr[n, 0:HEAD_DIM, :] = v_ref[rows, :].astype(F32).T.astype(BF16)
            vt_scr[n, HEAD_DIM:, :] = jnp.ones((ATTN_ONES_ROWS, MOBA_BLOCK), BF16)

    def fold8(t):
        return t.reshape(MOBA_BLOCK // SUBLANES, SUBLANES, MOBA_BLOCK)

    def rows_of(n):
        return slice(n * MOBA_BLOCK, (n + 1) * MOBA_BLOCK)

    class QueryBlock:
        def __init__(self, which, blk_idx, km_parts):
            self.idx = blk_idx
            self.base = which * (N_KV_BLOCKS + 1)
            self.q = q_ref[rows_of(which), :]
            self.out_rows = rows_of(which)
            lhs = jnp.concatenate([k_ref[rows_of(blk_idx), :], *km_parts], axis=0)
            res = lax.dot_general(lhs, self.q, _NT, preferred_element_type=F32)
            nb = N_KV_BLOCKS
            bs = (res[MOBA_BLOCK:MOBA_BLOCK + nb] + res[MOBA_BLOCK + nb:MOBA_BLOCK + 2 * nb]
                  + res[MOBA_BLOCK + 2 * nb:MOBA_BLOCK + 3 * nb])
            blk = lax.broadcasted_iota(jnp.int32, bs.shape, 0)
            past = blk < blk_idx
            bs = jnp.where(past, bs, NEG_INF)
            rank = jnp.zeros(bs.shape, F32)
            for m in range(N_KV_BLOCKS):
                row = bs[m:m + 1, :]
                tie = jnp.where(blk > m, 1.0, 0.0)
                rank = rank + jnp.where(row > bs, 1.0, jnp.where(row == bs, tie, 0.0))
            self.bias = jnp.where(past, jnp.where(rank < MOBA_TOPK, 0.0, NEG_INF), NEG_INF)
            s = res[0:MOBA_BLOCK]
            ki = lax.broadcasted_iota(jnp.int32, s.shape, 0)
            qi = lax.broadcasted_iota(jnp.int32, s.shape, 1)
            s = jnp.where(ki <= qi, s, NEG_INF)
            s_scr[rows_of(self.base + N_KV_BLOCKS), :] = s
            self.m8 = jnp.max(fold8(s), axis=0)
            self.acc = [None, None]

        def pass1_tile(self, n):
            sn = lax.dot_general(k_ref[rows_of(n), :], self.q, _NT, preferred_element_type=F32)
            s_scr[rows_of(self.base + n), :] = sn
            self.m8 = jnp.maximum(self.m8, jnp.max(fold8(sn), axis=0) + self.bias[n:n + 1, :])

        def finish_pass1(self):
            self.m = jnp.max(self.m8, axis=0, keepdims=True)
            self._accumulate(self.idx, N_KV_BLOCKS, self.m)

        def _accumulate(self, v_blk, slab, shift):
            p = jnp.exp2(s_scr[rows_of(self.base + slab), :] - shift)
            pv = jnp.dot(vt_scr[v_blk], p.astype(BF16), preferred_element_type=F32)
            k = v_blk % 2
            self.acc[k] = pv if self.acc[k] is None else self.acc[k] + pv

        def pass2_tile(self, n):
            self._accumulate(n, n, self.m - self.bias[n:n + 1, :])

        def store(self):
            acc = self.acc[0] if self.acc[1] is None else (
                self.acc[1] if self.acc[0] is None else self.acc[0] + self.acc[1])
            l = acc[HEAD_DIM:HEAD_DIM + 1, :]
            o_ref[self.out_rows, :] = (acc[0:HEAD_DIM, :] / l).T.astype(BF16)

    def attend_pair(pair):
        km = kmean_scr[...]
        km_hi = km.astype(BF16)
        rest = km - km_hi.astype(F32)
        km_mid = rest.astype(BF16)
        km_lo = (rest - km_mid.astype(F32)).astype(BF16)
        parts = (km_hi, km_mid, km_lo)
        a = QueryBlock(0, 2 * pair, parts)
        for n in range(a.idx):
            a.pass1_tile(n)
        a.finish_pass1()
        b = QueryBlock(1, 2 * pair + 1, parts)
        for n in range(b.idx):
            b.pass1_tile(n)
            if n < a.idx:
                a.pass2_tile(n)
        a.store()
        b.finish_pass1()
        for n in range(b.idx):
            b.pass2_tile(n)
        b.store()

    for pair in range(N_KV_BLOCKS // 2):
        pl.when(i == pair)(functools.partial(attend_pair, pair))


def _attention(proj):
    pairs = N_KV_BLOCKS // 2
    return pl.pallas_call(
        _attn_kernel,
        grid=(BATCH, N_HEADS, pairs),
        in_specs=[
            pl.BlockSpec((2 * MOBA_BLOCK, HEAD_DIM), lambda b, h, i: (b * pairs + i, h)),
            pl.BlockSpec((SEQ, HEAD_DIM), lambda b, h, i: (b, COL_K + h)),
            pl.BlockSpec((SEQ, HEAD_DIM), lambda b, h, i: (b, COL_V + h)),
        ],
        out_specs=pl.BlockSpec((2 * MOBA_BLOCK, HEAD_DIM), lambda b, h, i: (b * pairs + i, h)),
        out_shape=jax.ShapeDtypeStruct((TOKENS, ATTN_WIDTH), BF16),
        scratch_shapes=[pltpu.VMEM((N_KV_BLOCKS, HEAD_DIM), F32),
                        pltpu.VMEM((N_KV_BLOCKS, HEAD_DIM + ATTN_ONES_ROWS, MOBA_BLOCK), BF16),
                        pltpu.VMEM((2 * (N_KV_BLOCKS + 1) * MOBA_BLOCK, MOBA_BLOCK), F32)],
        compiler_params=_params("parallel", "parallel", "arbitrary"),
        name="moba_attn",
    )(proj, proj, proj)


def _s5_prep_kernel(lr_r, li_r, ldt_r, btr_ref, bti_ref, ctr_ref, cti_ref,
                    tm_ref, wst_ref, wout_ref, al_ref):
    L = S5_CHUNK
    ns = S5_TILE_STATES
    lr = lr_r[...]
    li = li_r[...]
    dt = jnp.exp(ldt_r[...])
    mag = jnp.exp(lr * dt)
    ab_re = mag * jnp.cos(li * dt)
    ab_im = mag * jnp.sin(li * dt)
    den = lr * lr + li * li
    nr = ab_re - 1.0
    ni = ab_im
    coef_re = (nr * lr + ni * li) / den
    coef_im = (ni * lr - nr * li) / den

    p_re = [jnp.ones_like(ab_re)]
    p_im = [jnp.zeros_like(ab_re)]
    for _ in range(L):
        p_re.append(p_re[-1] * ab_re - p_im[-1] * ab_im)
        p_im.append(p_re[-2] * ab_im + p_im[-1] * ab_re)

    btr = btr_ref[...]
    bti = bti_ref[...]
    ctr = ctr_ref[...]
    cti = cti_ref[...]

    ctr_hl = _split_bf16(ctr)
    cti_hl = _split_bf16(cti)
    tm_ref[...] = jnp.zeros(tm_ref.shape, tm_ref.dtype)
    for t in range(L):
        wr = coef_re * p_re[t] - coef_im * p_im[t]
        wi = coef_re * p_im[t] + coef_im * p_re[t]
        s_re = btr * wr - bti * wi
        s_im = btr * wi + bti * wr
        rp = L - 1 - t
        wst_ref[rp * LANES:(rp + 1) * LANES, 0:ns] = s_re.astype(BF16)
        wst_ref[rp * LANES:(rp + 1) * LANES, ns:2 * ns] = s_im.astype(BF16)
        kt = _dot_split(s_re, ctr_hl) - _dot_split(s_im, cti_hl)
        ktb = kt.astype(BF16)
        for r0 in range(L - t):
            tm_ref[r0 * LANES:(r0 + 1) * LANES, (r0 + t) * LANES:(r0 + t + 1) * LANES] = ktb

    al_ref[:, 0:ns] = p_re[L]
    al_ref[:, ns:2 * ns] = p_im[L]

    pad = jnp.zeros((LANES - L, ns), F32)
    a_re = jnp.concatenate(p_re[1:] + [pad], axis=0).T
    a_im = jnp.concatenate(p_im[1:] + [pad], axis=0).T
    for r in range(L):
        ar = a_re[:, r:r + 1]
        ai = a_im[:, r:r + 1]
        wout_ref[0:ns, r * LANES:(r + 1) * LANES] = (ctr * ar - cti * ai).astype(BF16)
        wout_ref[ns:2 * ns, r * LANES:(r + 1) * LANES] = (-(ctr * ai + cti * ar)).astype(BF16)


def _s5_prep(lr_r, li_r, ldt_r, btr, bti, ctr, cti):
    ns = S5_TILE_STATES
    row = pl.BlockSpec((None, 1, ns), lambda t: (t, 0, 0))
    bt = pl.BlockSpec((None, LANES, ns), lambda t: (t, 0, 0))
    ct = pl.BlockSpec((None, ns, LANES), lambda t: (t, 0, 0))
    return pl.pallas_call(
        _s5_prep_kernel,
        grid=(S5_TILES,),
        in_specs=[row, row, row, bt, bt, ct, ct],
        out_specs=[pl.BlockSpec((None, S5_FLAT, S5_FLAT), lambda t: (t, 0, 0)),
                   pl.BlockSpec((None, S5_FLAT, 2 * ns), lambda t: (t, 0, 0)),
                   pl.BlockSpec((None, 2 * ns, S5_FLAT), lambda t: (t, 0, 0)),
                   pl.BlockSpec((None, 1, 2 * ns), lambda t: (t, 0, 0))],
        out_shape=[jax.ShapeDtypeStruct((S5_TILES, S5_FLAT, S5_FLAT), BF16),
                   jax.ShapeDtypeStruct((S5_TILES, S5_FLAT, 2 * ns), BF16),
                   jax.ShapeDtypeStruct((S5_TILES, 2 * ns, S5_FLAT), BF16),
                   jax.ShapeDtypeStruct((S5_TILES, 1, 2 * ns), F32)],
        compiler_params=_params("parallel"),
        name="s5_prep",
    )(lr_r, li_r, ldt_r, btr, bti, ctr, cti)


def _s5_kernel(u_ref, tm_ref, wst_ref, wout_ref, al_ref, d_ref, *rest):
    n_cast = len(rest) // 2 - 3
    cast_in = rest[:n_cast]
    o_ref = rest[n_cast]
    cast_out = rest[n_cast + 1:2 * n_cast + 1]
    uf_scr, uflat_scr, xs_scr, xp_scr, y_scr = rest[2 * n_cast + 1:]
    _cast_slabs(zip(cast_in, cast_out))
    L = S5_CHUNK
    ns = S5_TILE_STATES
    uf_scr[...] = u_ref[...].astype(F32)
    for r in range(L):
        uflat_scr[:, r * LANES:(r + 1) * LANES] = uf_scr[pl.ds(r, S5_CHUNKS, stride=L), :].astype(BF16)
    uflat = uflat_scr[...]

    xs_scr[...] = jnp.dot(uflat, wst_ref[...], preferred_element_type=F32)
    al = al_ref[...]
    al_re = al[:, 0:ns]
    al_im = al[:, ns:2 * ns]

    def body(c, state):
        s_re, s_im = state
        xp_scr[pl.ds(c, 1), 0:ns] = s_re
        xp_scr[pl.ds(c, 1), ns:2 * ns] = s_im
        x_re = xs_scr[pl.ds(c, 1), 0:ns]
        x_im = xs_scr[pl.ds(c, 1), ns:2 * ns]
        return (al_re * s_re - al_im * s_im + x_re, al_re * s_im + al_im * s_re + x_im)

    zero = jnp.zeros((1, ns), F32)
    lax.fori_loop(0, S5_CHUNKS, body, (zero, zero))

    xp = xp_scr[...].astype(BF16)
    step = S5_FLAT // S5_TRI_SPLITS
    for c0 in range(0, S5_FLAT, step):
        c1 = c0 + step
        y_scr[:, c0:c1] = (
            jnp.dot(uflat_scr[:, 0:c1], tm_ref[0:c1, c0:c1], preferred_element_type=F32)
            + jnp.dot(xp, wout_ref[:, c0:c1], preferred_element_type=F32))
    d = d_ref[...]
    for r in range(L):
        ur = uf_scr[pl.ds(r, S5_CHUNKS, stride=L), :]
        yr = y_scr[:, r * LANES:(r + 1) * LANES] + d * ur
        uf_scr[pl.ds(r, S5_CHUNKS, stride=L), :] = jax.nn.gelu(yr, approximate=True)
    o_ref[...] = uf_scr[...].astype(BF16)


def _s5(proj, tmat, wst, wout, al, d_row, cast_weights):
    ns = S5_TILE_STATES
    w_specs, w_shapes = _slab_specs(cast_weights, S5_TILES * BATCH, lambda t, b: t * BATCH + b)
    return pl.pallas_call(
        _s5_kernel,
        grid=(S5_TILES, BATCH),
        in_specs=[
            pl.BlockSpec((SEQ, LANES), lambda t, b: (b, COL_U + t)),
            pl.BlockSpec((None, S5_FLAT, S5_FLAT), lambda t, b: (t, 0, 0)),
            pl.BlockSpec((None, S5_FLAT, 2 * ns), lambda t, b: (t, 0, 0)),
            pl.BlockSpec((None, 2 * ns, S5_FLAT), lambda t, b: (t, 0, 0)),
            pl.BlockSpec((None, 1, 2 * ns), lambda t, b: (t, 0, 0)),
            pl.BlockSpec((1, LANES), lambda t, b: (0, t)),
        ] + w_specs,
        out_specs=[pl.BlockSpec((SEQ, LANES), lambda t, b: (b, t))] + w_specs,
        out_shape=[jax.ShapeDtypeStruct((TOKENS, SSM_WIDTH), BF16)] + w_shapes,
        scratch_shapes=[pltpu.VMEM((SEQ, LANES), F32),
                        pltpu.VMEM((S5_CHUNKS, S5_FLAT), BF16),
                        pltpu.VMEM((S5_CHUNKS, 2 * ns), F32),
                        pltpu.VMEM((S5_CHUNKS, 2 * ns), F32),
                        pltpu.VMEM((S5_CHUNKS, S5_FLAT), F32)],
        compiler_params=_params("parallel", "parallel"),
        name="s5_scan",
    )(proj, tmat, wst, wout, al, d_row, *cast_weights)


def _merge_kernel(x_ref, gl_ref, ya_ref, ga_ref, gs_ref, g1_ref, wglu_ref, bglu_ref,
                  wa_ref, ws_ref, wo_ref, wup_ref, wdn_ref, o_ref, wup_bf_ref, wdn_bf_ref):
    _cast_slabs([(wup_ref, wup_bf_ref), (wdn_ref, wdn_bf_ref)])
    gl = gl_ref[...]
    z = jnp.dot(gl, wglu_ref[...], preferred_element_type=F32) + bglu_ref[...]
    y_ssm = (gl.astype(F32) * jax.nn.sigmoid(z)).astype(BF16)
    a = jnp.dot(ya_ref[...], wa_ref[...], preferred_element_type=F32)
    s = jnp.dot(y_ssm, ws_ref[...], preferred_element_type=F32)
    merged = (jax.nn.sigmoid(ga_ref[...].astype(F32)) * a
              + jax.nn.sigmoid(gs_ref[...].astype(F32)) * s)
    out = jnp.dot(merged.astype(BF16), wo_ref[...], preferred_element_type=F32)
    o_ref[...] = x_ref[...] + g1_ref[...] * out


def _merge(x2, gl, y_attn, proj, mod3, w_glu, b_glu, w_attn_br, w_ssm_br, w_out, w_up, w_down):
    tm = TM_MERGE
    tiles_per_seq = SEQ // tm
    steps = TOKENS // tm
    up_spec, up_shape = _slab_specs([w_up], steps, lambda i: i)
    dn_spec, dn_shape = _slab_specs([w_down], steps // 2, lambda i: i // 2)
    gate_blk = (3 * ATTN_WIDTH + SSM_WIDTH) // D_MODEL
    const = lambda shape: pl.BlockSpec(shape, lambda i: (0, 0), pipeline_mode=pl.Buffered(1))
    return pl.pallas_call(
        _merge_kernel,
        grid=(TOKENS // tm,),
        in_specs=[
            pl.BlockSpec((tm, D_MODEL), lambda i: (i, 0)),
            pl.BlockSpec((tm, SSM_WIDTH), lambda i: (i, 0)),
            pl.BlockSpec((tm, ATTN_WIDTH), lambda i: (i, 0)),
            pl.BlockSpec((tm, D_MODEL), lambda i: (i, gate_blk)),
            pl.BlockSpec((tm, D_MODEL), lambda i: (i, gate_blk + 1)),
            pl.BlockSpec((None, 1, D_MODEL), lambda i: (i // tiles_per_seq * N_MOD + 2, 0, 0)),
            const((SSM_WIDTH, SSM_WIDTH)),
            const((1, SSM_WIDTH)),
            const((ATTN_WIDTH, D_MODEL)),
            const((SSM_WIDTH, D_MODEL)),
            const((D_MODEL, D_MODEL)),
        ] + up_spec + dn_spec,
        out_specs=[pl.BlockSpec((tm, D_MODEL), lambda i: (i, 0))] + up_spec + dn_spec,
        out_shape=[jax.ShapeDtypeStruct((TOKENS, D_MODEL), F32)] + up_shape + dn_shape,
        compiler_params=_params("arbitrary"),
        name="merge",
    )(x2, gl, y_attn, proj, proj, mod3, w_glu, b_glu, w_attn_br, w_ssm_br, w_out, w_up, w_down)


def _ffn_kernel(x_ref, sc_ref, sh_ref, g2_ref, ng_ref, wv_ref, wg_ref,
                cwv_ref, cwg_ref, cbv_ref, cbg_ref, wd_ref, o_ref, h_scr, acc_scr, tail_scr):
    i = pl.program_id(0)
    j = pl.program_id(1)
    tm = TM_FFN

    @pl.when((i % (SEQ // tm)) == 0)
    def _():
        tail_scr[j] = jnp.zeros(tail_scr.shape[1:], F32)

    def conv(h, w_ref, cw_ref, cb_ref, which):
        up = jnp.dot(h, w_ref[...], preferred_element_type=F32)
        ext = jnp.concatenate([tail_scr[j, which], up], axis=0)
        tail_scr[j, which] = up[tm - FFN_TAIL:, :]
        cw = cw_ref[...]
        ext1 = pltpu.roll(ext, 1, axis=0)
        ext2 = pltpu.roll(ext, 2, axis=0)
        y = cw[2:3, :] * ext + cw[1:2, :] * ext1 + cw[0:1, :] * ext2 + cb_ref[...]
        return y[FFN_TAIL:, :]

    def contribution():
        h = h_scr[...]
        val = conv(h, wv_ref, cwv_ref, cbv_ref, 0)
        gt = conv(h, wg_ref, cwg_ref, cbg_ref, 1)
        act = (gt * jax.nn.sigmoid(gt) * val).astype(BF16)
        return jnp.dot(act, wd_ref[...], preferred_element_type=F32)

    @pl.when(j == 0)
    def _():
        h_scr[...] = _rms_modulate(x_ref[...], ng_ref[...], sc_ref[...], sh_ref[...]).astype(BF16)
        acc_scr[...] = contribution()

    @pl.when((j > 0) & (j < N_F_TILES - 1))
    def _():
        acc_scr[...] += contribution()

    @pl.when(j == N_F_TILES - 1)
    def _():
        o_ref[...] = x_ref[...] + g2_ref[...] * (acc_scr[...] + contribution())


def _ffn(x1, mod3, norm2_g, w_up, conv_w, conv_b, w_down):
    tm, tf = TM_FFN, TF_FFN
    tiles_per_seq = SEQ // tm
    modspec = lambda k: pl.BlockSpec((None, 1, D_MODEL),
                                     lambda i, j: (i // tiles_per_seq * N_MOD + k, 0, 0))
    return pl.pallas_call(
        _ffn_kernel,
        grid=(TOKENS // tm, N_F_TILES),
        in_specs=[
            pl.BlockSpec((tm, D_MODEL), lambda i, j: (i, 0)),
            modspec(4), modspec(3), modspec(5),
            pl.BlockSpec((1, D_MODEL), lambda i, j: (0, 0)),
            pl.BlockSpec((D_MODEL, tf), lambda i, j: (0, j)),
            pl.BlockSpec((D_MODEL, tf), lambda i, j: (0, N_F_TILES + j)),
            pl.BlockSpec((3, tf), lambda i, j: (0, j)),
            pl.BlockSpec((3, tf), lambda i, j: (0, N_F_TILES + j)),
            pl.BlockSpec((1, tf), lambda i, j: (0, j)),
            pl.BlockSpec((1, tf), lambda i, j: (0, N_F_TILES + j)),
            pl.BlockSpec((tf, D_MODEL), lambda i, j: (j, 0)),
        ],
        out_specs=pl.BlockSpec((tm, D_MODEL), lambda i, j: (i, 0)),
        out_shape=jax.ShapeDtypeStruct((TOKENS, D_MODEL), F32),
        scratch_shapes=[pltpu.VMEM((tm, D_MODEL), BF16),
                        pltpu.VMEM((tm, D_MODEL), F32),
                        pltpu.VMEM((N_F_TILES, 2, FFN_TAIL, tf), F32)],
        compiler_params=_params("arbitrary", "arbitrary"),
        name="ffn",
    )(x1, mod3, mod3, mod3, norm2_g, w_up, w_up, conv_w, conv_w, conv_b, conv_b, w_down)


def _block_diag(w):
    _, r, c = w.shape
    g = S5_TILE_GROUPS
    w = w.reshape(S5_TILES, g, r, 1, c)
    eye = jnp.eye(g, dtype=bool).reshape(1, g, 1, g, 1)
    return jnp.where(eye, w, 0.0).reshape(S5_TILES, g * r, g * c)


def _s5_branch(proj, lam_re, lam_im, log_dt, b_re, b_im, c_re, c_im, d_skip, cast_weights=()):
    ns = S5_TILE_STATES
    ldt = jnp.repeat(log_dt, SSM_STATE)
    rows = [a.reshape(S5_TILES, 1, ns) for a in (lam_re, lam_im, ldt)]
    btr = _block_diag(b_re.transpose(0, 2, 1))
    bti = _block_diag(b_im.transpose(0, 2, 1))
    ctr = _block_diag(c_re.transpose(0, 2, 1))
    cti = _block_diag(c_im.transpose(0, 2, 1))
    tmat, wst, wout, al = _s5_prep(*rows, btr, bti, ctr, cti)
    return _s5(proj, tmat, wst, wout, al, d_skip.reshape(1, -1), cast_weights)


def _layer(x2, c, pos_col, w_mod, b_mod, norm1_g, w_in, q_norm_g, k_norm_g,
           lam_re, lam_im, log_dt, b_re, b_im, c_re, c_im, d_skip, w_glu, b_glu,
           w_attn_br, w_ssm_br, w_out, norm2_g, w_up, conv_w, conv_b, w_down):
    c_pad = jnp.pad(c, ((0, SUBLANES - BATCH), (0, 0)))
    mod = _mod(c_pad, w_mod, b_mod.reshape(1, -1))[:BATCH]
    mod3 = mod.reshape(BATCH * N_MOD, 1, D_MODEL)

    half = HEAD_DIM // 2
    inv_freq = ROPE_THETA ** (-jnp.arange(half, dtype=F32) / half)
    freq_row = jnp.concatenate([inv_freq, inv_freq]).reshape(1, HEAD_DIM)
    cos, sin, w_in_bf = _rope_tables(pos_col, freq_row, w_in)

    proj = _in_proj(x2, mod3, norm1_g.reshape(1, -1), w_in_bf, cos, sin,
                    q_norm_g.reshape(1, -1), k_norm_g.reshape(1, -1))
    y_attn = _attention(proj)

    gl, w_glu_bf, w_attn_bf, w_ssm_bf, w_out_bf = _s5_branch(
        proj, lam_re, lam_im, log_dt, b_re, b_im, c_re, c_im, d_skip,
        cast_weights=(w_glu, w_attn_br, w_ssm_br, w_out))

    x1, w_up_bf, w_down_bf = _merge(x2, gl, y_attn, proj, mod3, w_glu_bf, b_glu.reshape(1, -1),
                                    w_attn_bf, w_ssm_bf, w_out_bf, w_up, w_down)
    return _ffn(x1, mod3, norm2_g.reshape(1, -1), w_up_bf, conv_w, conv_b.reshape(1, -1), w_down_bf)


def kernel(x, c, positions, w_mod, b_mod, norm1_g, w_in, q_norm_g, k_norm_g, ssm_lambda_re, ssm_lambda_im, ssm_log_dt, ssm_b_re, ssm_b_im, ssm_c_re, ssm_c_im, ssm_d, w_glu, b_glu, w_attn_br, w_ssm_br, w_out, norm2_g, w_up, conv_w, conv_b, w_down):
    assert x.shape == (BATCH, SEQ, D_MODEL) and w_in.shape[0] == 1
    x2 = x.reshape(TOKENS, D_MODEL)
    pos_col = positions.reshape(TOKENS, 1)
    for l in range(w_in.shape[0]):
        x2 = _layer(x2, c, pos_col, w_mod[l], b_mod[l], norm1_g[l], w_in[l], q_norm_g[l], k_norm_g[l],
                    ssm_lambda_re[l], ssm_lambda_im[l], ssm_log_dt[l], ssm_b_re[l], ssm_b_im[l],
                    ssm_c_re[l], ssm_c_im[l], ssm_d[l], w_glu[l], b_glu[l], w_attn_br[l],
                    w_ssm_br[l], w_out[l], norm2_g[l], w_up[l], conv_w[l], conv_b[l], w_down[l])
    return x2.reshape(BATCH, SEQ, D_MODEL)
```

```python
import functools
import math

import jax
import jax.numpy as jnp
from jax import lax
from jax.experimental import pallas as pl
from jax.experimental.pallas import tpu as pltpu

F32 = jnp.float32
BF16 = jnp.bfloat16

D_MODEL = 2048
BATCH = 4
SEQ = 4096
TOKENS = BATCH * SEQ
HEAD_DIM = 128
N_HEADS = 8
ATTN_WIDTH = N_HEADS * HEAD_DIM
MOBA_BLOCK = 256
N_KV_BLOCKS = SEQ // MOBA_BLOCK
MOBA_TOPK = 3
ROPE_THETA = 10000.0
SSM_WIDTH = D_MODEL // 2
SSM_GROUP = 16
SSM_GROUPS = SSM_WIDTH // SSM_GROUP
SSM_STATE = 64
FFN_HIDDEN = 5632
N_MOD = 6
IN_WIDTH = 3 * ATTN_WIDTH + SSM_WIDTH + 2 * D_MODEL
EPS = 1e-6
NEG_INF = -1e30

LANES = 128
SUBLANES = 8
VMEM_LIMIT = 56 * 1024 * 1024

TM_PROJ = 1024
TN_PROJ = 1024
ATTN_ONES_ROWS = 16
Q_SCALE = HEAD_DIM ** -0.5 * math.log2(math.e)
TM_MERGE = 256
TM_FFN = 512
TF_FFN = 512
N_F_TILES = FFN_HIDDEN // TF_FFN
FFN_TAIL = SUBLANES
S5_TILE_GROUPS = LANES // SSM_GROUP
S5_TILES = SSM_GROUPS // S5_TILE_GROUPS
S5_TILE_STATES = S5_TILE_GROUPS * SSM_STATE
S5_CHUNK = 16
S5_CHUNKS = SEQ // S5_CHUNK
S5_FLAT = S5_CHUNK * LANES
S5_TRI_SPLITS = 4

COL_K = ATTN_WIDTH // LANES
COL_V = 2 * ATTN_WIDTH // LANES
COL_U = 3 * ATTN_WIDTH // LANES


def _params(*sem):
    return pltpu.CompilerParams(dimension_semantics=sem, vmem_limit_bytes=VMEM_LIMIT)


def _split_bf16(x):
    hi = x.astype(BF16)
    return hi, (x - hi.astype(F32)).astype(BF16)


def _dot_split(a, b_hl):
    a_hi, a_lo = _split_bf16(a)
    b_hi, b_lo = b_hl
    dot = functools.partial(jnp.dot, preferred_element_type=F32)
    return dot(a_hi, b_hi) + (dot(a_hi, b_lo) + dot(a_lo, b_hi))


def _mod_kernel(c_ref, w_ref, b_ref, o_ref):
    c = c_ref[...]
    sc = c * jax.nn.sigmoid(c)
    o_ref[...] = _dot_split(sc, _split_bf16(w_ref[...])) + b_ref[...]


def _mod(c_pad, w_mod, b_mod):
    tn = 1024
    n = N_MOD * D_MODEL
    return pl.pallas_call(
        _mod_kernel,
        grid=(n // tn,),
        in_specs=[pl.BlockSpec((SUBLANES, D_MODEL), lambda j: (0, 0)),
                  pl.BlockSpec((D_MODEL, tn), lambda j: (0, j)),
                  pl.BlockSpec((1, tn), lambda j: (0, j))],
        out_specs=pl.BlockSpec((SUBLANES, tn), lambda j: (0, j)),
        out_shape=jax.ShapeDtypeStruct((SUBLANES, n), F32),
        compiler_params=_params("parallel"),
        name="mod",
    )(c_pad, w_mod, b_mod)


def _cast_slabs(refs):
    for src, dst in refs:
        dst[...] = src[...].astype(BF16)


def _slab_specs(weights, steps, index):
    specs, shapes = [], []
    for w in weights:
        rows, cols = w.shape
        slab = rows // steps
        assert slab * steps == rows and slab % (2 * SUBLANES) == 0, (w.shape, steps)
        specs.append(pl.BlockSpec((slab, cols), lambda *ids: (index(*ids), 0)))
        shapes.append(jax.ShapeDtypeStruct(w.shape, BF16))
    return specs, shapes


def _rope_kernel(pos_ref, freq_ref, w_ref, cos_ref, sin_ref, wbf_ref):
    ang = pos_ref[...].astype(F32) * freq_ref[...]
    lane = lax.broadcasted_iota(jnp.int32, ang.shape, 1)
    cos_ref[...] = jnp.cos(ang)
    sin_ref[...] = jnp.where(lane < HEAD_DIM // 2, -1.0, 1.0) * jnp.sin(ang)
    _cast_slabs([(w_ref, wbf_ref)])


def _rope_tables(pos_col, freq_row, w_in):
    tr = 2048
    steps = TOKENS // tr
    w_specs, w_shapes = _slab_specs([w_in], steps, lambda i: i)
    return pl.pallas_call(
        _rope_kernel,
        grid=(steps,),
        in_specs=[pl.BlockSpec((tr, 1), lambda i: (i, 0)),
                  pl.BlockSpec((1, HEAD_DIM), lambda i: (0, 0))] + w_specs,
        out_specs=[pl.BlockSpec((tr, HEAD_DIM), lambda i: (i, 0)),
                   pl.BlockSpec((tr, HEAD_DIM), lambda i: (i, 0))] + w_specs,
        out_shape=[jax.ShapeDtypeStruct((TOKENS, HEAD_DIM), F32)] * 2 + w_shapes,
        compiler_params=_params("parallel"),
        name="rope",
    )(pos_col, freq_row, w_in)


def _rms_modulate(x, g, scale, shift):
    ms = jnp.mean(x * x, axis=-1, keepdims=True)
    y = x * lax.rsqrt(ms + EPS) * g
    return y * (1.0 + scale) + shift


def _in_proj_kernel(x_ref, sc_ref, sh_ref, g_ref, w_ref, cos_ref, sin_ref, qg_ref, kg_ref,
                    o_ref, h_scr, raw_scr):
    j = pl.program_id(1)

    def project():
        return jnp.dot(h_scr[...], w_ref[...], preferred_element_type=F32)

    @pl.when(j == 0)
    def _():
        h_scr[...] = _rms_modulate(x_ref[...], g_ref[...], sc_ref[...], sh_ref[...]).astype(BF16)
        raw_scr[...] = project()

    @pl.when(j == 1)
    def _():
        raw_scr[...] = project()

    @pl.when(j < 2)
    def _():
        half = HEAD_DIM // 2
        g = jnp.where(j == 0, qg_ref[...], kg_ref[...])
        out_scale = jnp.where(j == 0, Q_SCALE, 1.0)
        c_tab = cos_ref[...] * (g * out_scale)
        s_tab = sin_ref[...] * (pltpu.roll(g, half, axis=1) * out_scale)
        for hd in range(N_HEADS):
            cols = slice(hd * HEAD_DIM, (hd + 1) * HEAD_DIM)
            t = raw_scr[:, cols]
            rs = lax.rsqrt(jnp.mean(t * t, axis=-1, keepdims=True) + EPS)
            o_ref[:, cols] = (rs * (t * c_tab + pltpu.roll(t, half, axis=1) * s_tab)).astype(BF16)

    @pl.when(j >= 2)
    def _():
        o_ref[...] = project().astype(BF16)


def _in_proj(x2, mod3, norm1_g, w_in_bf, cos, sin, qg, kg):
    tiles_per_seq = SEQ // TM_PROJ
    return pl.pallas_call(
        _in_proj_kernel,
        grid=(TOKENS // TM_PROJ, IN_WIDTH // TN_PROJ),
        in_specs=[
            pl.BlockSpec((TM_PROJ, D_MODEL), lambda i, j: (i, 0)),
            pl.BlockSpec((None, 1, D_MODEL), lambda i, j: (i // tiles_per_seq * N_MOD + 1, 0, 0)),
            pl.BlockSpec((None, 1, D_MODEL), lambda i, j: (i // tiles_per_seq * N_MOD + 0, 0, 0)),
            pl.BlockSpec((1, D_MODEL), lambda i, j: (0, 0)),
            pl.BlockSpec((D_MODEL, TN_PROJ), lambda i, j: (0, j)),
            pl.BlockSpec((TM_PROJ, HEAD_DIM), lambda i, j: (i, 0)),
            pl.BlockSpec((TM_PROJ, HEAD_DIM), lambda i, j: (i, 0)),
            pl.BlockSpec((1, HEAD_DIM), lambda i, j: (0, 0)),
            pl.BlockSpec((1, HEAD_DIM), lambda i, j: (0, 0)),
        ],
        out_specs=pl.BlockSpec((TM_PROJ, TN_PROJ), lambda i, j: (i, j)),
        out_shape=jax.ShapeDtypeStruct((TOKENS, IN_WIDTH), BF16),
        scratch_shapes=[pltpu.VMEM((TM_PROJ, D_MODEL), BF16),
                        pltpu.VMEM((TM_PROJ, TN_PROJ), F32)],
        compiler_params=_params("parallel", "arbitrary"),
        name="in_proj",
    )(x2, mod3, mod3, norm1_g, w_in_bf, cos, sin, qg, kg)


_NT = (((1,), (1,)), ((), ()))


def _attn_kernel(q_ref, k_ref, v_ref, o_ref, kmean_scr, vt_scr, s_scr):
    i = pl.program_id(2)

    @pl.when(i == 0)
    def _():
        for n in range(N_KV_BLOCKS):
            rows = slice(n * MOBA_BLOCK, (n + 1) * MOBA_BLOCK)
            kmean_scr[n:n + 1, :] = jnp.mean(k_ref[rows, :].astype(F32), axis=0, keepdims=True)
            vt_scr[n, 0:HEAD_DIM, :] = v_ref[rows, :].astype(F32).T.astype(BF16)
            vt_scr[n, HEAD_DIM:, :] = jnp.ones((ATTN_ONES_ROWS, MOBA_BLOCK), BF16)

    def fold8(t):
        return t.reshape(MOBA_BLOCK // SUBLANES, SUBLANES, MOBA_BLOCK)

    def rows_of(n):
        return slice(n * MOBA_BLOCK, (n + 1) * MOBA_BLOCK)

    class QueryBlock:
        def __init__(self, which, blk_idx, km_parts):
            self.idx = blk_idx
            self.base = which * (N_KV_BLOCKS + 1)
            self.q = q_ref[rows_of(which), :]
            self.out_rows = rows_of(which)
            lhs = jnp.concatenate([k_ref[rows_of(blk_idx), :], *km_parts], axis=0)
            res = lax.dot_general(lhs, self.q, _NT, preferred_element_type=F32)
            nb = N_KV_BLOCKS
            bs = (res[MOBA_BLOCK:MOBA_BLOCK + nb] + res[MOBA_BLOCK + nb:MOBA_BLOCK + 2 * nb]
                  + res[MOBA_BLOCK + 2 * nb:MOBA_BLOCK + 3 * nb])
            blk = lax.broadcasted_iota(jnp.int32, bs.shape, 0)
            past = blk < blk_idx
            bs = jnp.where(past, bs, NEG_INF)
            rank = jnp.zeros(bs.shape, F32)
            for m in range(N_KV_BLOCKS):
                row = bs[m:m + 1, :]
                tie = jnp.where(blk > m, 1.0, 0.0)
                rank = rank + jnp.where(row > bs, 1.0, jnp.where(row == bs, tie, 0.0))
            self.bias = jnp.where(past, jnp.where(rank < MOBA_TOPK, 0.0, NEG_INF), NEG_INF)
            s = res[0:MOBA_BLOCK]
            ki = lax.broadcasted_iota(jnp.int32, s.shape, 0)
            qi = lax.broadcasted_iota(jnp.int32, s.shape, 1)
            s = jnp.where(ki <= qi, s, NEG_INF)
            s_scr[rows_of(self.base + N_KV_BLOCKS), :] = s
            self.m8 = jnp.max(fold8(s), axis=0)
            self.acc = [None, None]

        def pass1_tile(self, n):
            sn = lax.dot_general(k_ref[rows_of(n), :], self.q, _NT, preferred_element_type=F32)
            s_scr[rows_of(self.base + n), :] = sn
            self.m8 = jnp.maximum(self.m8, jnp.max(fold8(sn), axis=0) + self.bias[n:n + 1, :])

        def finish_pass1(self):
            self.m = jnp.max(self.m8, axis=0, keepdims=True)
            self._accumulate(self.idx, N_KV_BLOCKS, self.m)

        def _accumulate(self, v_blk, slab, shift):
            p = jnp.exp2(s_scr[rows_of(self.base + slab), :] - shift)
            pv = jnp.dot(vt_scr[v_blk], p.astype(BF16), preferred_element_type=F32)
            k = v_blk % 2
            self.acc[k] = pv if self.acc[k] is None else self.acc[k] + pv

        def pass2_tile(self, n):
            self._accumulate(n, n, self.m - self.bias[n:n + 1, :])

        def store(self):
            acc = self.acc[0] if self.acc[1] is None else (
                self.acc[1] if self.acc[0] is None else self.acc[0] + self.acc[1])
            l = acc[HEAD_DIM:HEAD_DIM + 1, :]
            o_ref[self.out_rows, :] = (acc[0:HEAD_DIM, :] / l).T.astype(BF16)

    def attend_pair(pair):
        km = kmean_scr[...]
        km_hi = km.astype(BF16)
        rest = km - km_hi.astype(F32)
        km_mid = rest.astype(BF16)
        km_lo = (rest - km_mid.astype(F32)).astype(BF16)
        parts = (km_hi, km_mid, km_lo)
        a = QueryBlock(0, 2 * pair, parts)
        for n in range(a.idx):
            a.pass1_tile(n)
        a.finish_pass1()
        b = QueryBlock(1, 2 * pair + 1, parts)
        for n in range(b.idx):
            b.pass1_tile(n)
            if n < a.idx:
                a.pass2_tile(n)
        a.store()
        b.finish_pass1()
        for n in range(b.idx):
            b.pass2_tile(n)
        b.store()

    for pair in range(N_KV_BLOCKS // 2):
        pl.when(i == pair)(functools.partial(attend_pair, pair))


def _attention(proj):
    pairs = N_KV_BLOCKS // 2
    return pl.pallas_call(
        _attn_kernel,
        grid=(BATCH, N_HEADS, pairs),
        in_specs=[
            pl.BlockSpec((2 * MOBA_BLOCK, HEAD_DIM), lambda b, h, i: (b * pairs + i, h)),
            pl.BlockSpec((SEQ, HEAD_DIM), lambda b, h, i: (b, COL_K + h)),
            pl.BlockSpec((SEQ, HEAD_DIM), lambda b, h, i: (b, COL_V + h)),
        ],
        out_specs=pl.BlockSpec((2 * MOBA_BLOCK, HEAD_DIM), lambda b, h, i: (b * pairs + i, h)),
        out_shape=jax.ShapeDtypeStruct((TOKENS, ATTN_WIDTH), BF16),
        scratch_shapes=[pltpu.VMEM((N_KV_BLOCKS, HEAD_DIM), F32),
                        pltpu.VMEM((N_KV_BLOCKS, HEAD_DIM + ATTN_ONES_ROWS, MOBA_BLOCK), BF16),
                        pltpu.VMEM((2 * (N_KV_BLOCKS + 1) * MOBA_BLOCK, MOBA_BLOCK), F32)],
        compiler_params=_params("parallel", "parallel", "arbitrary"),
        name="moba_attn",
    )(proj, proj, proj)


def _s5_prep_kernel(lr_r, li_r, ldt_r, btr_ref, bti_ref, ctr_ref, cti_ref,
                    tm_ref, wst_ref, wout_ref, al_ref):
    L = S5_CHUNK
    ns = S5_TILE_STATES
    lr = lr_r[...]
    li = li_r[...]
    dt = jnp.exp(ldt_r[...])
    mag = jnp.exp(lr * dt)
    ab_re = mag * jnp.cos(li * dt)
    ab_im = mag * jnp.sin(li * dt)
    den = lr * lr + li * li
    nr = ab_re - 1.0
    ni = ab_im
    coef_re = (nr * lr + ni * li) / den
    coef_im = (ni * lr - nr * li) / den

    p_re = [jnp.ones_like(ab_re)]
    p_im = [jnp.zeros_like(ab_re)]
    for _ in range(L):
        p_re.append(p_re[-1] * ab_re - p_im[-1] * ab_im)
        p_im.append(p_re[-2] * ab_im + p_im[-1] * ab_re)

    btr = btr_ref[...]
    bti = bti_ref[...]
    ctr = ctr_ref[...]
    cti = cti_ref[...]

    ctr_hl = _split_bf16(ctr)
    cti_hl = _split_bf16(cti)
    tm_ref[...] = jnp.zeros(tm_ref.shape, tm_ref.dtype)
    for t in range(L):
        wr = coef_re * p_re[t] - coef_im * p_im[t]
        wi = coef_re * p_im[t] + coef_im * p_re[t]
        s_re = btr * wr - bti * wi
        s_im = btr * wi + bti * wr
        rp = L - 1 - t
        wst_ref[rp * LANES:(rp + 1) * LANES, 0:ns] = s_re.astype(BF16)
        wst_ref[rp * LANES:(rp + 1) * LANES, ns:2 * ns] = s_im.astype(BF16)
        kt = _dot_split(s_re, ctr_hl) - _dot_split(s_im, cti_hl)
        ktb = kt.astype(BF16)
        for r0 in range(L - t):
            tm_ref[r0 * LANES:(r0 + 1) * LANES, (r0 + t) * LANES:(r0 + t + 1) * LANES] = ktb

    al_ref[:, 0:ns] = p_re[L]
    al_ref[:, ns:2 * ns] = p_im[L]

    pad = jnp.zeros((LANES - L, ns), F32)
    a_re = jnp.concatenate(p_re[1:] + [pad], axis=0).T
    a_im = jnp.concatenate(p_im[1:] + [pad], axis=0).T
    for r in range(L):
        ar = a_re[:, r:r + 1]
        ai = a_im[:, r:r + 1]
        wout_ref[0:ns, r * LANES:(r + 1) * LANES] = (ctr * ar - cti * ai).astype(BF16)
        wout_ref[ns:2 * ns, r * LANES:(r + 1) * LANES] = (-(ctr * ai + cti * ar)).astype(BF16)


def _s5_prep(lr_r, li_r, ldt_r, btr, bti, ctr, cti):
    ns = S5_TILE_STATES
    row = pl.BlockSpec((None, 1, ns), lambda t: (t, 0, 0))
    bt = pl.BlockSpec((None, LANES, ns), lambda t: (t, 0, 0))
    ct = pl.BlockSpec((None, ns, LANES), lambda t: (t, 0, 0))
    return pl.pallas_call(
        _s5_prep_kernel,
        grid=(S5_TILES,),
        in_specs=[row, row, row, bt, bt, ct, ct],
        out_specs=[pl.BlockSpec((None, S5_FLAT, S5_FLAT), lambda t: (t, 0, 0)),
                   pl.BlockSpec((None, S5_FLAT, 2 * ns), lambda t: (t, 0, 0)),
                   pl.BlockSpec((None, 2 * ns, S5_FLAT), lambda t: (t, 0, 0)),
                   pl.BlockSpec((None, 1, 2 * ns), lambda t: (t, 0, 0))],
        out_shape=[jax.ShapeDtypeStruct((S5_TILES, S5_FLAT, S5_FLAT), BF16),
                   jax.ShapeDtypeStruct((S5_TILES, S5_FLAT, 2 * ns), BF16),
                   jax.ShapeDtypeStruct((S5_TILES, 2 * ns, S5_FLAT), BF16),
                   jax.ShapeDtypeStruct((S5_TILES, 1, 2 * ns), F32)],
        compiler_params=_params("parallel"),
        name="s5_prep",
    )(lr_r, li_r, ldt_r, btr, bti, ctr, cti)


def _s5_kernel(u_ref, tm_ref, wst_ref, wout_ref, al_ref, d_ref, *rest):
    n_cast = len(rest) // 2 - 3
    cast_in = rest[:n_cast]
    o_ref = rest[n_cast]
    cast_out = rest[n_cast + 1:2 * n_cast + 1]
    uf_scr, uflat_scr, xs_scr, xp_scr, y_scr = rest[2 * n_cast + 1:]
    _cast_slabs(zip(cast_in, cast_out))
    L = S5_CHUNK
    ns = S5_TILE_STATES
    uf_scr[...] = u_ref[...].astype(F32)
    for r in range(L):
        uflat_scr[:, r * LANES:(r + 1) * LANES] = uf_scr[pl.ds(r, S5_CHUNKS, stride=L), :].astype(BF16)
    uflat = uflat_scr[...]

    xs_scr[...] = jnp.dot(uflat, wst_ref[...], preferred_element_type=F32)
    al = al_ref[...]
    al_re = al[:, 0:ns]
    al_im = al[:, ns:2 * ns]

    def body(c, state):
        s_re, s_im = state
        xp_scr[pl.ds(c, 1), 0:ns] = s_re
        xp_scr[pl.ds(c, 1), ns:2 * ns] = s_im
        x_re = xs_scr[pl.ds(c, 1), 0:ns]
        x_im = xs_scr[pl.ds(c, 1), ns:2 * ns]
        return (al_re * s_re - al_im * s_im + x_re, al_re * s_im + al_im * s_re + x_im)

    zero = jnp.zeros((1, ns), F32)
    lax.fori_loop(0, S5_CHUNKS, body, (zero, zero))

    xp = xp_scr[...].astype(BF16)
    step = S5_FLAT // S5_TRI_SPLITS
    for c0 in range(0, S5_FLAT, step):
        c1 = c0 + step
        y_scr[:, c0:c1] = (
            jnp.dot(uflat_scr[:, 0:c1], tm_ref[0:c1, c0:c1], preferred_element_type=F32)
            + jnp.dot(xp, wout_ref[:, c0:c1], preferred_element_type=F32))
    d = d_ref[...]
    for r in range(L):
        ur = uf_scr[pl.ds(r, S5_CHUNKS, stride=L), :]
        yr = y_scr[:, r * LANES:(r + 1) * LANES] + d * ur
        uf_scr[pl.ds(r, S5_CHUNKS, stride=L), :] = jax.nn.gelu(yr, approximate=True)
    o_ref[...] = uf_scr[...].astype(BF16)


def _s5(proj, tmat, wst, wout, al, d_row, cast_weights):
    ns = S5_TILE_STATES
    w_specs, w_shapes = _slab_specs(cast_weights, S5_TILES * BATCH, lambda t, b: t * BATCH + b)
    return pl.pallas_call(
        _s5_kernel,
        grid=(S5_TILES, BATCH),
        in_specs=[
            pl.BlockSpec((SEQ, LANES), lambda t, b: (b, COL_U + t)),
            pl.BlockSpec((None, S5_FLAT, S5_FLAT), lambda t, b: (t, 0, 0)),
            pl.BlockSpec((None, S5_FLAT, 2 * ns), lambda t, b: (t, 0, 0)),
            pl.BlockSpec((None, 2 * ns, S5_FLAT), lambda t, b: (t, 0, 0)),
            pl.BlockSpec((None, 1, 2 * ns), lambda t, b: (t, 0, 0)),
            pl.BlockSpec((1, LANES), lambda t, b: (0, t)),
        ] + w_specs,
        out_specs=[pl.BlockSpec((SEQ, LANES), lambda t, b: (b, t))] + w_specs,
        out_shape=[jax.ShapeDtypeStruct((TOKENS, SSM_WIDTH), BF16)] + w_shapes,
        scratch_shapes=[pltpu.VMEM((SEQ, LANES), F32),
                        pltpu.VMEM((S5_CHUNKS, S5_FLAT), BF16),
                        pltpu.VMEM((S5_CHUNKS, 2 * ns), F32),
                        pltpu.VMEM((S5_CHUNKS, 2 * ns), F32),
                        pltpu.VMEM((S5_CHUNKS, S5_FLAT), F32)],
        compiler_params=_params("parallel", "parallel"),
        name="s5_scan",
    )(proj, tmat, wst, wout, al, d_row, *cast_weights)


def _merge_kernel(x_ref, gl_ref, ya_ref, ga_ref, gs_ref, g1_ref, wglu_ref, bglu_ref,
                  wa_ref, ws_ref, wo_ref, wup_ref, wdn_ref, o_ref, wup_bf_ref, wdn_bf_ref):
    _cast_slabs([(wup_ref, wup_bf_ref), (wdn_ref, wdn_bf_ref)])
    gl = gl_ref[...]
    z = jnp.dot(gl, wglu_ref[...], preferred_element_type=F32) + bglu_ref[...]
    y_ssm = (gl.astype(F32) * jax.nn.sigmoid(z)).astype(BF16)
    a = jnp.dot(ya_ref[...], wa_ref[...], preferred_element_type=F32)
    s = jnp.dot(y_ssm, ws_ref[...], preferred_element_type=F32)
    merged = (jax.nn.sigmoid(ga_ref[...].astype(F32)) * a
              + jax.nn.sigmoid(gs_ref[...].astype(F32)) * s)
    out = jnp.dot(merged.astype(BF16), wo_ref[...], preferred_element_type=F32)
    o_ref[...] = x_ref[...] + g1_ref[...] * out


def _merge(x2, gl, y_attn, proj, mod3, w_glu, b_glu, w_attn_br, w_ssm_br, w_out, w_up, w_down):
    tm = TM_MERGE
    tiles_per_seq = SEQ // tm
    steps = TOKENS // tm
    up_spec, up_shape = _slab_specs([w_up], steps, lambda i: i)
    dn_spec, dn_shape = _slab_specs([w_down], steps // 2, lambda i: i // 2)
    gate_blk = (3 * ATTN_WIDTH + SSM_WIDTH) // D_MODEL
    const = lambda shape: pl.BlockSpec(shape, lambda i: (0, 0), pipeline_mode=pl.Buffered(1))
    return pl.pallas_call(
        _merge_kernel,
        grid=(TOKENS // tm,),
        in_specs=[
            pl.BlockSpec((tm, D_MODEL), lambda i: (i, 0)),
            pl.BlockSpec((tm, SSM_WIDTH), lambda i: (i, 0)),
            pl.BlockSpec((tm, ATTN_WIDTH), lambda i: (i, 0)),
            pl.BlockSpec((tm, D_MODEL), lambda i: (i, gate_blk)),
            pl.BlockSpec((tm, D_MODEL), lambda i: (i, gate_blk + 1)),
            pl.BlockSpec((None, 1, D_MODEL), lambda i: (i // tiles_per_seq * N_MOD + 2, 0, 0)),
            const((SSM_WIDTH, SSM_WIDTH)),
            const((1, SSM_WIDTH)),
            const((ATTN_WIDTH, D_MODEL)),
            const((SSM_WIDTH, D_MODEL)),
            const((D_MODEL, D_MODEL)),
        ] + up_spec + dn_spec,
        out_specs=[pl.BlockSpec((tm, D_MODEL), lambda i: (i, 0))] + up_spec + dn_spec,
        out_shape=[jax.ShapeDtypeStruct((TOKENS, D_MODEL), F32)] + up_shape + dn_shape,
        compiler_params=_params("arbitrary"),
        name="merge",
    )(x2, gl, y_attn, proj, proj, mod3, w_glu, b_glu, w_attn_br, w_ssm_br, w_out, w_up, w_down)


def _ffn_kernel(x_ref, sc_ref, sh_ref, g2_ref, ng_ref, wv_ref, wg_ref,
                cwv_ref, cwg_ref, cbv_ref, cbg_ref, wd_ref, o_ref, h_scr, acc_scr, tail_scr):
    i = pl.program_id(0)
    j = pl.program_id(1)
    tm = TM_FFN

    @pl.when((i % (SEQ // tm)) == 0)
    def _():
        tail_scr[j] = jnp.zeros(tail_scr.shape[1:], F32)

    def conv(h, w_ref, cw_ref, cb_ref, which):
        up = jnp.dot(h, w_ref[...], preferred_element_type=F32)
        ext = jnp.concatenate([tail_scr[j, which], up], axis=0)
        tail_scr[j, which] = up[tm - FFN_TAIL:, :]
        cw = cw_ref[...]
        ext1 = pltpu.roll(ext, 1, axis=0)
        ext2 = pltpu.roll(ext, 2, axis=0)
        y = cw[2:3, :] * ext + cw[1:2, :] * ext1 + cw[0:1, :] * ext2 + cb_ref[...]
        return y[FFN_TAIL:, :]

    def contribution():
        h = h_scr[...]
        val = conv(h, wv_ref, cwv_ref, cbv_ref, 0)
        gt = conv(h, wg_ref, cwg_ref, cbg_ref, 1)
        act = (gt * jax.nn.sigmoid(gt) * val).astype(BF16)
        return jnp.dot(act, wd_ref[...], preferred_element_type=F32)

    @pl.when(j == 0)
    def _():
        h_scr[...] = _rms_modulate(x_ref[...], ng_ref[...], sc_ref[...], sh_ref[...]).astype(BF16)
        acc_scr[...] = contribution()

    @pl.when((j > 0) & (j < N_F_TILES - 1))
    def _():
        acc_scr[...] += contribution()

    @pl.when(j == N_F_TILES - 1)
    def _():
        o_ref[...] = x_ref[...] + g2_ref[...] * (acc_scr[...] + contribution())


def _ffn(x1, mod3, norm2_g, w_up, conv_w, conv_b, w_down):
    tm, tf = TM_FFN, TF_FFN
    tiles_per_seq = SEQ // tm
    modspec = lambda k: pl.BlockSpec((None, 1, D_MODEL),
                                     lambda i, j: (i // tiles_per_seq * N_MOD + k, 0, 0))
    return pl.pallas_call(
        _ffn_kernel,
        grid=(TOKENS // tm, N_F_TILES),
        in_specs=[
            pl.BlockSpec((tm, D_MODEL), lambda i, j: (i, 0)),
            modspec(4), modspec(3), modspec(5),
            pl.BlockSpec((1, D_MODEL), lambda i, j: (0, 0)),
            pl.BlockSpec((D_MODEL, tf), lambda i, j: (0, j)),
            pl.BlockSpec((D_MODEL, tf), lambda i, j: (0, N_F_TILES + j)),
            pl.BlockSpec((3, tf), lambda i, j: (0, j)),
            pl.BlockSpec((3, tf), lambda i, j: (0, N_F_TILES + j)),
            pl.BlockSpec((1, tf), lambda i, j: (0, j)),
            pl.BlockSpec((1, tf), lambda i, j: (0, N_F_TILES + j)),
            pl.BlockSpec((tf, D_MODEL), lambda i, j: (j, 0)),
        ],
        out_specs=pl.BlockSpec((tm, D_MODEL), lambda i, j: (i, 0)),
        out_shape=jax.ShapeDtypeStruct((TOKENS, D_MODEL), F32),
        scratch_shapes=[pltpu.VMEM((tm, D_MODEL), BF16),
                        pltpu.VMEM((tm, D_MODEL), F32),
                        pltpu.VMEM((N_F_TILES, 2, FFN_TAIL, tf), F32)],
        compiler_params=_params("arbitrary", "arbitrary"),
        name="ffn",
    )(x1, mod3, mod3, mod3, norm2_g, w_up, w_up, conv_w, conv_w, conv_b, conv_b, w_down)


def _block_diag(w):
    _, r, c = w.shape
    g = S5_TILE_GROUPS
    w = w.reshape(S5_TILES, g, r, 1, c)
    eye = jnp.eye(g, dtype=bool).reshape(1, g, 1, g, 1)
    return jnp.where(eye, w, 0.0).reshape(S5_TILES, g * r, g * c)


def _s5_branch(proj, lam_re, lam_im, log_dt, b_re, b_im, c_re, c_im, d_skip, cast_weights=()):
    ns = S5_TILE_STATES
    ldt = jnp.repeat(log_dt, SSM_STATE)
    rows = [a.reshape(S5_TILES, 1, ns) for a in (lam_re, lam_im, ldt)]
    btr = _block_diag(b_re.transpose(0, 2, 1))
    bti = _block_diag(b_im.transpose(0, 2, 1))
    ctr = _block_diag(c_re.transpose(0, 2, 1))
    cti = _block_diag(c_im.transpose(0, 2, 1))
    tmat, wst, wout, al = _s5_prep(*rows, btr, bti, ctr, cti)
    return _s5(proj, tmat, wst, wout, al, d_skip.reshape(1, -1), cast_weights)


def _layer(x2, c, pos_col, w_mod, b_mod, norm1_g, w_in, q_norm_g, k_norm_g,
           lam_re, lam_im, log_dt, b_re, b_im, c_re, c_im, d_skip, w_glu, b_glu,
           w_attn_br, w_ssm_br, w_out, norm2_g, w_up, conv_w, conv_b, w_down):
    c_pad = jnp.pad(c, ((0, SUBLANES - BATCH), (0, 0)))
    mod = _mod(c_pad, w_mod, b_mod.reshape(1, -1))[:BATCH]
    mod3 = mod.reshape(BATCH * N_MOD, 1, D_MODEL)

    half = HEAD_DIM // 2
    inv_freq = ROPE_THETA ** (-jnp.arange(half, dtype=F32) / half)
    freq_row = jnp.concatenate([inv_freq, inv_freq]).reshape(1, HEAD_DIM)
    cos, sin, w_in_bf = _rope_tables(pos_col, freq_row, w_in)

    proj = _in_proj(x2, mod3, norm1_g.reshape(1, -1), w_in_bf, cos, sin,
                    q_norm_g.reshape(1, -1), k_norm_g.reshape(1, -1))
    y_attn = _attention(proj)

    gl, w_glu_bf, w_attn_bf, w_ssm_bf, w_out_bf = _s5_branch(
        proj, lam_re, lam_im, log_dt, b_re, b_im, c_re, c_im, d_skip,
        cast_weights=(w_glu, w_attn_br, w_ssm_br, w_out))

    x1, w_up_bf, w_down_bf = _merge(x2, gl, y_attn, proj, mod3, w_glu_bf, b_glu.reshape(1, -1),
                                    w_attn_bf, w_ssm_bf, w_out_bf, w_up, w_down)
    return _ffn(x1, mod3, norm2_g.reshape(1, -1), w_up_bf, conv_w, conv_b.reshape(1, -1), w_down_bf)


def kernel(x, c, positions, w_mod, b_mod, norm1_g, w_in, q_norm_g, k_norm_g, ssm_lambda_re, ssm_lambda_im, ssm_log_dt, ssm_b_re, ssm_b_im, ssm_c_re, ssm_c_im, ssm_d, w_glu, b_glu, w_attn_br, w_ssm_br, w_out, norm2_g, w_up, conv_w, conv_b, w_down):
    assert x.shape == (BATCH, SEQ, D_MODEL) and w_in.shape[0] == 1
    x2 = x.reshape(TOKENS, D_MODEL)
    pos_col = positions.reshape(TOKENS, 1)
    for l in range(w_in.shape[0]):
        x2 = _layer(x2, c, pos_col, w_mod[l], b_mod[l], norm1_g[l], w_in[l], q_norm_g[l], k_norm_g[l],
                    ssm_lambda_re[l], ssm_lambda_im[l], ssm_log_dt[l], ssm_b_re[l], ssm_b_im[l],
                    ssm_c_re[l], ssm_c_im[l], ssm_d[l], w_glu[l], b_glu[l], w_attn_br[l],
                    w_ssm_br[l], w_out[l], norm2_g[l], w_up[l], conv_w[l], conv_b[l], w_down[l])
    return x2.reshape(BATCH, SEQ, D_MODEL)
```
